```python
import math
import jax, jax.numpy as jnp
from jax import lax
import numpy as np

D_MODEL = 2048
BATCH = 4
SEQ = 4096
DEPTH = 2

GRID_W = 64
CTX_LEN = 256
HEAD_DIM = 128
A_Q_HEADS = 8
A_KV_HEADS = 2
B_Q_HEADS = 8
B_KV_HEADS = 2
WINDOW = 128
Q_BLOCK = 128
ROPE_THETA = 10000.0
QK_NORM_EPS = 1e-6
LN_EPS = 1e-5
ATTN_IN = (A_Q_HEADS + 2 * A_KV_HEADS + B_Q_HEADS + 2 * B_KV_HEADS) * HEAD_DIM
MIX_WIDTH = (A_Q_HEADS + B_Q_HEADS) * HEAD_DIM
HY_ORDER = 2
HY_SHORT = 3
HY_EMB = 33
HY_FILTER_WIDTH = 64
HY_SIN_FREQ = 1.0
HY_FILTER_INIT = 0.02
HY_DECAY_FAST = 0.3
HY_DECAY_SLOW = 1.5
HY_DECAY_TARGET = 1e-2
N_EXPERTS = 64
EXPERT_FF = 512
SHARED_FF = 512
TOP_K = 8
N_GROUPS = 8
TOP_GROUPS = 4
ROUTED_SCALE = 2.5
MOE_BLOCK = 128
N_ATTN_LAYERS = (DEPTH + 1) // 2
N_HYENA_LAYERS = DEPTH // 2
DEEPNORM_ALPHA = (2 * DEPTH) ** 0.25
DEEPNORM_BETA = (8 * DEPTH) ** -0.25

kernel_name = 'hybrid_attn_hyena_moe_diffusion_trunk'

F32 = jnp.float32


def layer_norm(x, g, b):
    xf = x.astype(F32)
    mu = jnp.mean(xf, -1, keepdims=True)
    var = jnp.mean(jnp.square(xf - mu), -1, keepdims=True)
    return ((xf - mu) * lax.rsqrt(var + LN_EPS) * g.astype(F32) + b.astype(F32)).astype(x.dtype)


def rms_norm(x, g):
    xf = x.astype(F32)
    return (xf * lax.rsqrt(jnp.mean(xf * xf, -1, keepdims=True) + QK_NORM_EPS) * g.astype(F32)).astype(x.dtype)


def axial_angles(n_tokens):
    rows = n_tokens // GRID_W
    row = jnp.repeat(jnp.arange(rows, dtype=F32), GRID_W)
    col = jnp.tile(jnp.arange(GRID_W, dtype=F32), rows)
    n_freq = HEAD_DIM // 4
    inv = ROPE_THETA ** (-jnp.arange(n_freq, dtype=F32) / n_freq)
    return row[:, None] * inv, col[:, None] * inv


def _rotate_half(x, ang):
    x1, x2 = jnp.split(x, 2, axis=-1)
    cos = jnp.cos(ang)[None, :, None, :]
    sin = jnp.sin(ang)[None, :, None, :]
    return jnp.concatenate([x1 * cos - x2 * sin, x2 * cos + x1 * sin], axis=-1)


def apply_axial_rope(x, ang_r, ang_c):
    xr, xc = jnp.split(x.astype(F32), 2, axis=-1)
    return jnp.concatenate([_rotate_half(xr, ang_r), _rotate_half(xc, ang_c)], axis=-1).astype(x.dtype)


def split_heads(t, n_heads):
    return t.reshape(t.shape[0], t.shape[1], n_heads, HEAD_DIM)


def windowed_sink_attention(q, k, v, kc, vc, sink):
    B, S, Hq, dh = q.shape
    Hkv = k.shape[2]
    G = Hq // Hkv
    nb = S // Q_BLOCK
    scale = dh ** -0.5
    qb = q.reshape(B, nb, Q_BLOCK, Hkv, G, dh)
    pad = ((0, 0), (Q_BLOCK, Q_BLOCK), (0, 0), (0, 0))

    def band(t):
        tb = jnp.pad(t, pad).reshape(B, nb + 2, Q_BLOCK, Hkv, dh)
        return jnp.concatenate([tb[:, :-2], tb[:, 1:-1], tb[:, 2:]], axis=2)

    kw, vw = band(k), band(v)
    s_loc = jnp.einsum('bnqhgd,bnkhd->bnhgqk', qb, kw, preferred_element_type=F32) * scale
    s_ctx = jnp.einsum('bnqhgd,bchd->bnhgqc', qb, kc, preferred_element_type=F32) * scale
    qpos = jnp.arange(nb)[:, None] * Q_BLOCK + jnp.arange(Q_BLOCK)[None, :]
    kpos = (jnp.arange(nb)[:, None] - 1) * Q_BLOCK + jnp.arange(3 * Q_BLOCK)[None, :]
    valid = ((jnp.abs(qpos[:, :, None] - kpos[:, None, :]) <= WINDOW)
             & (kpos[:, None, :] >= 0) & (kpos[:, None, :] < S))
    s_loc = jnp.where(valid[None, :, None, None], s_loc, -jnp.inf)
    sink_l = jnp.broadcast_to(sink.astype(F32).reshape(1, 1, Hkv, G, 1, 1), s_loc.shape[:-1] + (1,))
    p = jax.nn.softmax(jnp.concatenate([s_loc, s_ctx, sink_l], axis=-1), axis=-1)
    n_loc = 3 * Q_BLOCK
    C = kc.shape[1]
    p_loc = p[..., :n_loc].astype(v.dtype)
    p_ctx = p[..., n_loc:n_loc + C].astype(v.dtype)
    o = (jnp.einsum('bnhgqk,bnkhd->bnqhgd', p_loc, vw)
         + jnp.einsum('bnhgqc,bchd->bnqhgd', p_ctx, vc))
    return o.reshape(B, S, Hq * dh)


def dense_block_attention(q, k_all, v_all):
    B, S, Hq, dh = q.shape
    Hkv = k_all.shape[2]
    G = Hq // Hkv
    nb = S // Q_BLOCK
    scale = dh ** -0.5
    qb = q.reshape(B, nb, Q_BLOCK, Hkv, G, dh).transpose(1, 0, 2, 3, 4, 5)

    def one_block(qblk):
        s = jnp.einsum('bqhgd,bkhd->bhgqk', qblk, k_all, preferred_element_type=F32) * scale
        p = jax.nn.softmax(s, axis=-1).astype(v_all.dtype)
        return jnp.einsum('bhgqk,bkhd->bqhgd', p, v_all)

    o = lax.map(one_block, qb)
    return o.transpose(1, 0, 2, 3, 4, 5).reshape(B, S, Hq * dh)


def context_attention(q, k, v, sink=None):
    B, C, Hq, dh = q.shape
    Hkv = k.shape[2]
    G = Hq // Hkv
    qg = q.reshape(B, C, Hkv, G, dh)
    s = jnp.einsum('bqhgd,bkhd->bhgqk', qg, k, preferred_element_type=F32) * dh ** -0.5
    if sink is not None:
        s = jnp.concatenate([s, jnp.broadcast_to(sink.astype(F32).reshape(1, Hkv, G, 1, 1), s.shape[:-1] + (1,))], -1)
    p = jax.nn.softmax(s, axis=-1)[..., :C].astype(v.dtype)
    return jnp.einsum('bhgqk,bkhd->bqhgd', p, v).reshape(B, C, Hq * dh)


def attention_mixer(h, hc, w_in, w_out, sink, q_gain, k_gain, ang_r, ang_c, ctx_live):
    sizes = (A_Q_HEADS * HEAD_DIM, A_KV_HEADS * HEAD_DIM, A_KV_HEADS * HEAD_DIM,
             B_Q_HEADS * HEAD_DIM, B_KV_HEADS * HEAD_DIM, B_KV_HEADS * HEAD_DIM)
    splits = [sum(sizes[:i + 1]) for i in range(len(sizes) - 1)]
    qa, ka, va, qb, kb, vb = jnp.split(h @ w_in, splits, axis=-1)
    qa = apply_axial_rope(split_heads(qa, A_Q_HEADS), ang_r, ang_c)
    ka = apply_axial_rope(split_heads(ka, A_KV_HEADS), ang_r, ang_c)
    va = split_heads(va, A_KV_HEADS)
    qb = apply_axial_rope(rms_norm(split_heads(qb, B_Q_HEADS), q_gain), ang_r, ang_c)
    kb = apply_axial_rope(rms_norm(split_heads(kb, B_KV_HEADS), k_gain), ang_r, ang_c)
    vb = split_heads(vb, B_KV_HEADS)
    wp = jnp.split(w_in, splits, axis=1)
    ka_c = split_heads(hc @ wp[1], A_KV_HEADS)
    va_c = split_heads(hc @ wp[2], A_KV_HEADS)
    kb_c = rms_norm(split_heads(hc @ wp[4], B_KV_HEADS), k_gain)
    vb_c = split_heads(hc @ wp[5], B_KV_HEADS)
    o_a = windowed_sink_attention(qa, ka, va, ka_c, va_c, sink)
    o_b = dense_block_attention(qb, jnp.concatenate([kb, kb_c], 1), jnp.concatenate([vb, vb_c], 1))
    y = jnp.concatenate([o_a, o_b], axis=-1) @ w_out
    y_ctx = None
    if ctx_live:
        qa_c = split_heads(hc @ wp[0], A_Q_HEADS)
        qb_c = rms_norm(split_heads(hc @ wp[3], B_Q_HEADS), q_gain)
        y_ctx = jnp.concatenate([context_attention(qa_c, ka_c, va_c, sink),
                                 context_attention(qb_c, kb_c, vb_c)], axis=-1) @ w_out
    return y, y_ctx


def hyena_filter_spectrum(L, w1, b1, w2, b2, w3, b3, w4, freq):
    t = jnp.linspace(0.0, 1.0, L, dtype=F32)[:, None]
    bands = (HY_EMB - 1) // 2
    w = (2.0 * math.pi / L) * jnp.arange(L, dtype=F32)
    fb = jnp.linspace(1e-4, bands - 1, bands, dtype=F32)
    fw = w[:, None] * fb[None, :]
    z = jnp.concatenate([t, jnp.cos(fw), -jnp.sin(fw)], axis=-1)
    fq = freq.astype(F32)
    a = jnp.sin(fq * (z @ w1.astype(F32) + b1.astype(F32)))
    a = jnp.sin(fq * (a @ w2.astype(F32) + b2.astype(F32)))
    a = jnp.sin(fq * (a @ w3.astype(F32) + b3.astype(F32)))
    hf = (a @ w4.astype(F32)).reshape(L, 2, HY_ORDER, -1)
    D = hf.shape[-1]
    deltas = jnp.linspace(math.log(HY_DECAY_TARGET) / HY_DECAY_FAST,
                          math.log(HY_DECAY_TARGET) / HY_DECAY_SLOW, D, dtype=F32)
    hf = hf * jnp.exp(-t * jnp.abs(deltas))[:, None, None, :]
    fwd, bwd = hf[:, 0], hf[:, 1]
    kern = jnp.concatenate([fwd, jnp.zeros_like(fwd[:1]), bwd[1:][::-1]], axis=0)
    return jnp.fft.rfft(kern, axis=0)


def long_conv(z, kf, bias):
    L = z.shape[1]
    zf = z.astype(F32)
    y = jnp.fft.irfft(jnp.fft.rfft(zf, n=2 * L, axis=1) * kf[None], n=2 * L, axis=1)[:, :L]
    return (y + zf * bias.astype(F32)).astype(z.dtype)


def hyena_mixer(h, w_in, conv_w, conv_b, f_w1, f_b1, f_w2, f_b2, f_w3, f_b3, f_w4, f_freq, bias_d, w_out):
    L = h.shape[1]
    u = h @ w_in
    n_ch = u.shape[-1]
    u = lax.conv_general_dilated(u, conv_w[:, None, :], window_strides=(1,), padding=[(1, 1)],
                                 dimension_numbers=('NWC', 'WIO', 'NWC'),
                                 feature_group_count=n_ch) + conv_b
    v, x1, x2 = jnp.split(u, 3, axis=-1)
    kf = hyena_filter_spectrum(L, f_w1, f_b1, f_w2, f_b2, f_w3, f_b3, f_w4, f_freq)
    z = v
    for n, gate in enumerate((x1, x2)):
        z = gate * long_conv(z, kf[:, n], bias_d[n])
    return z @ w_out


def moe(x, router_w, router_b, w_gate, w_up, w_down, s_gate, s_up, s_down):
    T, D = x.shape
    scores = jax.nn.sigmoid(x.astype(F32) @ router_w.astype(F32))
    biased = scores + router_b.astype(F32)
    grp_score = lax.top_k(biased.reshape(T, N_GROUPS, -1), 2)[0].sum(-1)
    top_grp = lax.top_k(grp_score, TOP_GROUPS)[1]
    grp_mask = jax.nn.one_hot(top_grp, N_GROUPS, dtype=F32).sum(1) > 0
    masked = jnp.where(jnp.repeat(grp_mask, N_EXPERTS // N_GROUPS, axis=1), biased, -jnp.inf)
    top_e = lax.top_k(masked, TOP_K)[1]
    gw = jnp.take_along_axis(scores, top_e, axis=1)
    gw = gw / jnp.sum(gw, -1, keepdims=True) * ROUTED_SCALE
    TK = T * TOP_K
    flat_e = top_e.reshape(-1)
    flat_tok = jnp.repeat(jnp.arange(T, dtype=jnp.int32), TOP_K)
    order = jnp.argsort(flat_e, stable=True)
    e_s = flat_e[order]
    counts = jnp.bincount(flat_e, length=N_EXPERTS)
    starts = jnp.cumsum(counts) - counts
    pcounts = (counts + MOE_BLOCK - 1) // MOE_BLOCK * MOE_BLOCK
    pends = jnp.cumsum(pcounts)
    pstarts = pends - pcounts
    dest = pstarts[e_s] + jnp.arange(TK) - starts[e_s]
    n_blocks = -(-TK // MOE_BLOCK) + N_EXPERTS
    n_rows = n_blocks * MOE_BLOCK
    row_tok = jnp.full((n_rows,), T, jnp.int32).at[dest].set(flat_tok[order])
    row_w = jnp.zeros((n_rows,), F32).at[dest].set(gw.reshape(-1)[order])
    blk_e = jnp.minimum(jnp.searchsorted(pends, jnp.arange(n_blocks) * MOE_BLOCK, side='right'), N_EXPERTS - 1)
    x_pad = jnp.concatenate([x, jnp.zeros((1, D), x.dtype)], axis=0)

    def expert_block(args):
        tok, wt, e = args
        xb = x_pad[tok]
        hid = jax.nn.silu(xb @ w_gate[e]) * (xb @ w_up[e])
        return (hid @ w_down[e]) * wt[:, None].astype(x.dtype)

    yb = lax.map(expert_block, (row_tok.reshape(n_blocks, MOE_BLOCK),
                                row_w.reshape(n_blocks, MOE_BLOCK), blk_e))
    routed = jnp.zeros((T + 1, D), x.dtype).at[row_tok].add(yb.reshape(n_rows, D))[:T]
    shared = (jax.nn.silu(x @ s_gate) * (x @ s_up)) @ s_down
    return routed + shared


def setup_inputs(seed: int = 0) -> dict:
    key = jax.random.key(seed)
    ks = list(jax.random.split(key, 48))
    it = iter(ks)

    def nrm(shape, scale):
        return jax.random.normal(next(it), shape, F32) * scale

    D = D_MODEL
    NA, NH = N_ATTN_LAYERS, N_HYENA_LAYERS
    E, FF, SF = N_EXPERTS, EXPERT_FF, SHARED_FF
    return {
        'x': nrm((BATCH, SEQ, D), 1.0),
        'c': nrm((BATCH, D), 1.0),
        'ctx': nrm((BATCH, CTX_LEN, D), 1.0),
        'c_ctx': nrm((D,), 1.0),
        'ada_w': nrm((DEPTH, D, 6 * D), 0.5 * D ** -0.5),
        'ada_b': nrm((DEPTH, 6 * D), 0.02),
        'ln_g': 1.0 + nrm((DEPTH, 2, D), 0.02),
        'ln_b': nrm((DEPTH, 2, D), 0.02),
        'attn_w_in': nrm((NA, D, ATTN_IN), D ** -0.5),
        'attn_w_out': nrm((NA, MIX_WIDTH, D), DEEPNORM_BETA * MIX_WIDTH ** -0.5),
        'attn_sink': nrm((NA, A_Q_HEADS), 0.5),
        'attn_q_gain': 1.0 + nrm((NA, HEAD_DIM), 0.02),
        'attn_k_gain': 1.0 + nrm((NA, HEAD_DIM), 0.02),
        'hy_w_in': nrm((NH, D, 3 * D), D ** -0.5),
        'hy_conv_w': nrm((NH, HY_SHORT, 3 * D), HY_SHORT ** -0.5),
        'hy_conv_b': nrm((NH, 3 * D), 0.02),
        'hy_f_w1': nrm((NH, HY_EMB, HY_FILTER_WIDTH), HY_EMB ** -0.5),
        'hy_f_b1': nrm((NH, HY_FILTER_WIDTH), 0.1),
        'hy_f_w2': nrm((NH, HY_FILTER_WIDTH, HY_FILTER_WIDTH), HY_FILTER_WIDTH ** -0.5),
        'hy_f_b2': nrm((NH, HY_FILTER_WIDTH), 0.1),
        'hy_f_w3': nrm((NH, HY_FILTER_WIDTH, HY_FILTER_WIDTH), HY_FILTER_WIDTH ** -0.5),
        'hy_f_b3': nrm((NH, HY_FILTER_WIDTH), 0.1),
        'hy_f_w4': nrm((NH, HY_FILTER_WIDTH, 2 * HY_ORDER * D), HY_FILTER_INIT * HY_FILTER_WIDTH ** -0.5),
        'hy_f_freq': HY_SIN_FREQ + nrm((NH, HY_FILTER_WIDTH), 0.02),
        'hy_bias_d': nrm((NH, HY_ORDER, D), 0.5),
        'hy_w_out': nrm((NH, D, D), DEEPNORM_BETA * D ** -0.5),
        'moe_router_w': nrm((DEPTH, D, E), D ** -0.5),
        'moe_router_b': nrm((DEPTH, E), 0.01),
        'moe_w_gate': nrm((DEPTH, E, D, FF), D ** -0.5),
        'moe_w_up': nrm((DEPTH, E, D, FF), D ** -0.5),
        'moe_w_down': nrm((DEPTH, E, FF, D), DEEPNORM_BETA * FF ** -0.5),
        'moe_s_gate': nrm((DEPTH, D, SF), D ** -0.5),
        'moe_s_up': nrm((DEPTH, D, SF), D ** -0.5),
        'moe_s_down': nrm((DEPTH, SF, D), DEEPNORM_BETA * SF ** -0.5),
    }


def reference(x, c, ctx, c_ctx, ada_w, ada_b, ln_g, ln_b,
              attn_w_in, attn_w_out, attn_sink, attn_q_gain, attn_k_gain,
              hy_w_in, hy_conv_w, hy_conv_b, hy_f_w1, hy_f_b1, hy_f_w2, hy_f_b2,
              hy_f_w3, hy_f_b3, hy_f_w4, hy_f_freq, hy_bias_d, hy_w_out,
              moe_router_w, moe_router_b, moe_w_gate, moe_w_up, moe_w_down,
              moe_s_gate, moe_s_up, moe_s_down):
    B, S, D = x.shape
    C = ctx.shape[1]
    ang_r, ang_c = axial_angles(S)
    for l in range(DEPTH):
        ctx_live = any(j % 2 == 0 for j in range(l + 1, DEPTH))
        m = jnp.split((jax.nn.silu(c) @ ada_w[l] + ada_b[l])[:, None, :], 6, axis=-1)
        mc = jnp.split((jax.nn.silu(c_ctx) @ ada_w[l] + ada_b[l])[None, None, :], 6, axis=-1)
        h = x * (1.0 + m[1]) + m[0]
        hc = ctx * (1.0 + mc[1]) + mc[0]
        i = l // 2
        if l % 2 == 0:
            y, y_ctx = attention_mixer(h, hc, attn_w_in[i], attn_w_out[i], attn_sink[i],
                                       attn_q_gain[i], attn_k_gain[i], ang_r, ang_c, ctx_live)
        else:
            hy = (hy_w_in[i], hy_conv_w[i], hy_conv_b[i], hy_f_w1[i], hy_f_b1[i], hy_f_w2[i], hy_f_b2[i],
                  hy_f_w3[i], hy_f_b3[i], hy_f_w4[i], hy_f_freq[i], hy_bias_d[i], hy_w_out[i])
            y = hyena_mixer(h, *hy)
            y_ctx = hyena_mixer(hc, *hy) if ctx_live else None
        x = layer_norm(DEEPNORM_ALPHA * x + m[2] * y, ln_g[l, 0], ln_b[l, 0])
        if ctx_live:
            ctx = layer_norm(DEEPNORM_ALPHA * ctx + mc[2] * y_ctx, ln_g[l, 0], ln_b[l, 0])
        h = (x * (1.0 + m[4]) + m[3]).reshape(B * S, D)
        if ctx_live:
            h = jnp.concatenate([h, (ctx * (1.0 + mc[4]) + mc[3]).reshape(B * C, D)], axis=0)
        out = moe(h, moe_router_w[l], moe_router_b[l], moe_w_gate[l], moe_w_up[l], moe_w_down[l],
                  moe_s_gate[l], moe_s_up[l], moe_s_down[l])
        x = layer_norm(DEEPNORM_ALPHA * x + m[5] * out[:B * S].reshape(B, S, D), ln_g[l, 1], ln_b[l, 1])
        if ctx_live:
            ctx = layer_norm(DEEPNORM_ALPHA * ctx + mc[5] * out[B * S:].reshape(B, C, D), ln_g[l, 1], ln_b[l, 1])
    return x
```

```python
import functools
import math

import numpy as np
import jax
import jax.numpy as jnp
from jax import lax
from jax.experimental import pallas as pl
from jax.experimental.pallas import tpu as pltpu

F32 = jnp.float32
BF16 = jnp.bfloat16

DEPTH = 2
GRID_W = 64
HEAD_DIM = 128
A_Q_HEADS, A_KV_HEADS, B_Q_HEADS, B_KV_HEADS = 8, 2, 8, 2
GQA_GROUP = A_Q_HEADS // A_KV_HEADS
WINDOW = 128
Q_BLOCK = 128
ROPE_THETA = 10000.0
QK_NORM_EPS = 1e-6
LN_EPS = 1e-5
HY_ORDER = 2
HY_EMB = 33
HY_FILTER_WIDTH = 64
HY_DECAY_FAST, HY_DECAY_SLOW, HY_DECAY_TARGET = 0.3, 1.5, 1e-2
N_EXPERTS = 64
TOP_K = 8
N_GROUPS = 8
TOP_GROUPS = 4
ROUTED_SCALE = 2.5
DEEPNORM_ALPHA = (2 * DEPTH) ** 0.25

LANES = 128
SUBLANES = 8
VMEM_LIMIT = 56 * 1024 * 1024

ADA_TN = 1024
INPROJ_TM = 512
DENSE_TQ = 256
DENSE_TK = 512
OUTPROJ_TM = 256
HY_TM = 512
CONV_TC = 512
MOE_TM = 256
COMB_TM = 128
FFT_N1, FFT_N2 = 64, 128
FFT_K1 = FFT_N1 // 2 + 1
FFT_K1P = 40
FFT_TL = 16384
FILT_TM = 512


def _cparams(*sem):
    return pltpu.CompilerParams(dimension_semantics=sem, vmem_limit_bytes=VMEM_LIMIT)


def _silu(v):
    return v * jax.nn.sigmoid(v)


def _split_bf16(v):
    hi = v.astype(BF16)
    lo = (v - hi.astype(F32)).astype(BF16)
    return hi, lo


def _dot(a, b):
    return jnp.dot(a, b, preferred_element_type=F32)


def _dot_nt(a, b):
    return lax.dot_general(a, b, (((1,), (1,)), ((), ())), preferred_element_type=F32)


def _dot3(a, b):
    ah, al = _split_bf16(a)
    bh, bl = _split_bf16(b)
    return _dot(ah, bh) + _dot(al, bh) + _dot(ah, bl)


def _ada_kernel(c_ref, w_ref, b_ref, o_ref):
    s = _silu(c_ref[...]).astype(BF16)
    o_ref[...] = _dot(s, w_ref[...].astype(BF16)) + b_ref[...]


def _ada_modulation(cc, ada_w, ada_b):
    depth, d, n = ada_w.shape
    rows = cc.shape[0]
    return pl.pallas_call(
        _ada_kernel,
        out_shape=jax.ShapeDtypeStruct((depth, rows, n), F32),
        grid=(depth, n // ADA_TN),
        in_specs=[
            pl.BlockSpec((rows, d), lambda l, j: (0, 0)),
            pl.BlockSpec((None, d, ADA_TN), lambda l, j: (l, 0, j)),
            pl.BlockSpec((None, 1, ADA_TN), lambda l, j: (l, 0, j)),
        ],
        out_specs=pl.BlockSpec((None, rows, ADA_TN), lambda l, j: (l, 0, j)),
        compiler_params=_cparams("arbitrary", "arbitrary"),
        name="ada_modulation",
    )(cc, ada_w, ada_b.reshape(depth, 1, n))


_N_SLOTS = (A_Q_HEADS + 2 * A_KV_HEADS + B_Q_HEADS + 2 * B_KV_HEADS)
_QA0, _KA0, _VA0 = 0, A_Q_HEADS, A_Q_HEADS + A_KV_HEADS
_QB0 = A_Q_HEADS + 2 * A_KV_HEADS
_KB0, _VB0 = _QB0 + B_Q_HEADS, _QB0 + B_Q_HEADS + B_KV_HEADS


def _slot_kind(j):
    if j < _KA0:
        return "qa"
    if j < _VA0:
        return "ka"
    if j < _QB0:
        return "v"
    if j < _KB0:
        return "qb"
    if j < _VB0:
        return "kb"
    return "v"


def _attn_inproj_kernel(x_ref, mod_ref, w_ref, cos_ref, sin_ref, qg_ref, kg_ref, o_ref):
    x = x_ref[...]
    h = (x * (1.0 + mod_ref[1:2, :]) + mod_ref[0:1, :]).astype(BF16)
    y = _dot(h, w_ref[...])
    cos = cos_ref[...]
    sin = sin_ref[...]
    lane = lax.broadcasted_iota(jnp.int32, cos.shape, 1)
    low_half = (lane & (HEAD_DIM // 2 - 1)) < (HEAD_DIM // 4)
    q_scale = HEAD_DIM ** -0.5
    for j in range(_N_SLOTS):
        kind = _slot_kind(j)
        blk = y[:, j * HEAD_DIM:(j + 1) * HEAD_DIM]
        if kind in ("qb", "kb"):
            gain = qg_ref[...] if kind == "qb" else kg_ref[...]
            ms = jnp.mean(blk * blk, axis=-1, keepdims=True)
            blk = blk * lax.rsqrt(ms + QK_NORM_EPS) * gain
        if kind != "v":
            partner = jnp.where(low_half,
                                pltpu.roll(blk, HEAD_DIM - HEAD_DIM // 4, 1),
                                pltpu.roll(blk, HEAD_DIM // 4, 1))
            blk = blk * cos + partner * sin
        if kind in ("qa", "qb"):
            blk = blk * q_scale
        o_ref[:, j * HEAD_DIM:(j + 1) * HEAD_DIM] = blk.astype(o_ref.dtype)


def _attn_inproj(x2d, mod, w_bf16, cos_t, sin_t, q_gain, k_gain, rows_per_mod, mod_base, tm):
    rows, d = x2d.shape
    n = w_bf16.shape[1]
    tiles_per_mod = rows_per_mod // tm
    tab_tiles = cos_t.shape[0] // tm
    return pl.pallas_call(
        _attn_inproj_kernel,
        out_shape=jax.ShapeDtypeStruct((rows, n), BF16),
        grid=(rows // tm,),
        in_specs=[
            pl.BlockSpec((tm, d), lambda i: (i, 0)),
            pl.BlockSpec((None, SUBLANES, d), lambda i: (mod_base + i // tiles_per_mod, 0, 0)),
            pl.BlockSpec((d, n), lambda i: (0, 0), pipeline_mode=pl.Buffered(1)),
            pl.BlockSpec((tm, HEAD_DIM), lambda i: (i % tab_tiles, 0)),
            pl.BlockSpec((tm, HEAD_DIM), lambda i: (i % tab_tiles, 0)),
            pl.BlockSpec((1, HEAD_DIM), lambda i: (0, 0)),
            pl.BlockSpec((1, HEAD_DIM), lambda i: (0, 0)),
        ],
        out_specs=pl.BlockSpec((tm, n), lambda i: (i, 0)),
        compiler_params=_cparams("arbitrary"),
        name="attn_inproj",
    )(x2d, mod, w_bf16, cos_t, sin_t, q_gain.reshape(1, -1), k_gain.reshape(1, -1))


def _rope_tables(seq):
    rows = seq // GRID_W
    row = jnp.repeat(jnp.arange(rows, dtype=F32), GRID_W)
    col = jnp.tile(jnp.arange(GRID_W, dtype=F32), rows)
    n_freq = HEAD_DIM // 4
    inv = ROPE_THETA ** (-jnp.arange(n_freq, dtype=F32) / n_freq)
    ang_r, ang_c = row[:, None] * inv, col[:, None] * inv
    cos = jnp.concatenate([jnp.cos(ang_r)] * 2 + [jnp.cos(ang_c)] * 2, axis=-1)
    sin = jnp.concatenate([-jnp.sin(ang_r), jnp.sin(ang_r), -jnp.sin(ang_c), jnp.sin(ang_c)], axis=-1)
    return cos, sin


def _win_attn_kernel(sink_ref, q_ref, k_ref, v_ref, kc_ref, vc_ref, o_ref):
    seq = q_ref.shape[0]
    hk = pl.program_id(1)
    n_loc = 3 * Q_BLOCK
    rows = GQA_GROUP * Q_BLOCK
    row = lax.broadcasted_iota(jnp.int32, (rows, 1), 0)
    col = lax.broadcasted_iota(jnp.int32, (1, n_loc), 1)
    sink_col = jnp.full((rows, 1), sink_ref[hk * GQA_GROUP + GQA_GROUP - 1], F32)
    for g in range(GQA_GROUP - 2, -1, -1):
        sink_col = jnp.where(row < (g + 1) * Q_BLOCK, sink_ref[hk * GQA_GROUP + g], sink_col)
    kc = kc_ref[...]
    vc = vc_ref[...]

    def body(n, carry):
        q0 = pl.multiple_of(n * Q_BLOCK, Q_BLOCK)
        start = pl.multiple_of(jnp.clip(q0 - Q_BLOCK, 0, seq - n_loc), Q_BLOCK)
        q = jnp.concatenate(
            [q_ref[pl.ds(q0, Q_BLOCK), g * HEAD_DIM:(g + 1) * HEAD_DIM] for g in range(GQA_GROUP)], axis=0)
        kw = k_ref[pl.ds(start, n_loc), :]
        vw = v_ref[pl.ds(start, n_loc), :]
        s_loc = _dot_nt(q, kw)
        s_ctx = _dot_nt(q, kc)
        qpos = q0 + (row & (Q_BLOCK - 1))
        kpos = start + col
        s_loc = jnp.where(jnp.abs(qpos - kpos) <= WINDOW, s_loc, -jnp.inf)
        m = jnp.maximum(jnp.maximum(jnp.max(s_loc, -1, keepdims=True), jnp.max(s_ctx, -1, keepdims=True)),
                        sink_col)
        p_loc = jnp.exp(s_loc - m)
        p_ctx = jnp.exp(s_ctx - m)
        denom = (jnp.sum(p_loc, -1, keepdims=True) + jnp.sum(p_ctx, -1, keepdims=True)
                 + jnp.exp(sink_col - m))
        o = (_dot(p_loc.astype(BF16), vw) + _dot(p_ctx.astype(BF16), vc)) / denom
        for g in range(GQA_GROUP):
            o_ref[pl.ds(q0, Q_BLOCK), g * HEAD_DIM:(g + 1) * HEAD_DIM] = (
                o[g * Q_BLOCK:(g + 1) * Q_BLOCK].astype(o_ref.dtype))
        return carry

    lax.fori_loop(0, seq // Q_BLOCK, body, 0)


def _win_attention(qkv, qkv_ctx, sink):
    b, s, _ = qkv.shape
    c = qkv_ctx.shape[1]
    gw = GQA_GROUP * HEAD_DIM
    return pl.pallas_call(
        _win_attn_kernel,
        out_shape=jax.ShapeDtypeStruct((b, s, A_Q_HEADS * HEAD_DIM), BF16),
        grid=(b, A_KV_HEADS),
        in_specs=[
            pl.BlockSpec(memory_space=pltpu.SMEM),
            pl.BlockSpec((None, s, gw), lambda i, h: (i, 0, h)),
            pl.BlockSpec((None, s, HEAD_DIM), lambda i, h: (i, 0, _KA0 + h)),
            pl.BlockSpec((None, s, HEAD_DIM), lambda i, h: (i, 0, _VA0 + h)),
            pl.BlockSpec((None, c, HEAD_DIM), lambda i, h: (i, 0, _KA0 + h)),
            pl.BlockSpec((None, c, HEAD_DIM), lambda i, h: (i, 0, _VA0 + h)),
        ],
        out_specs=pl.BlockSpec((None, s, gw), lambda i, h: (i, 0, h)),
        compiler_params=_cparams("arbitrary", "arbitrary"),
        name="win_attention",
    )(sink, qkv, qkv, qkv, qkv_ctx, qkv_ctx)


def _dense_attn_kernel(q_ref, k_ref, v_ref, kc_ref, vc_ref, o_ref, m_sc, l_sc, acc_sc):
    seq = k_ref.shape[0]
    tq = q_ref.shape[0]
    q = jnp.concatenate([q_ref[:, g * HEAD_DIM:(g + 1) * HEAD_DIM] for g in range(GQA_GROUP)], axis=0)
    m_sc[...] = jnp.full(m_sc.shape, -jnp.inf, F32)
    l_sc[...] = jnp.zeros(l_sc.shape, F32)
    acc_sc[...] = jnp.zeros(acc_sc.shape, F32)

    def step(kb, vb):
        s = _dot_nt(q, kb)
        m_prev = m_sc[...]
        m_new = jnp.maximum(m_prev, jnp.max(s, -1, keepdims=True))
        alpha = jnp.exp(m_prev - m_new)
        p = jnp.exp(s - m_new)
        l_sc[...] = alpha * l_sc[...] + jnp.sum(p, -1, keepdims=True)
        acc_sc[...] = alpha * acc_sc[...] + _dot(p.astype(BF16), vb)
        m_sc[...] = m_new

    def body(j, carry):
        k0 = pl.multiple_of(j * DENSE_TK, DENSE_TK)
        step(k_ref[pl.ds(k0, DENSE_TK), :], v_ref[pl.ds(k0, DENSE_TK), :])
        return carry

    lax.fori_loop(0, seq // DENSE_TK, body, 0)
    step(kc_ref[...], vc_ref[...])
    o = acc_sc[...] / l_sc[...]
    for g in range(GQA_GROUP):
        o_ref[:, g * HEAD_DIM:(g + 1) * HEAD_DIM] = o[g * tq:(g + 1) * tq].astype(o_ref.dtype)


def _dense_attention(qkv, qkv_ctx):
    b, s, _ = qkv.shape
    c = qkv_ctx.shape[1]
    gw = GQA_GROUP * HEAD_DIM
    qb_blk = _QB0 * HEAD_DIM // gw
    rows = GQA_GROUP * DENSE_TQ
    return pl.pallas_call(
        _dense_attn_kernel,
        out_shape=jax.ShapeDtypeStruct((b, s, B_Q_HEADS * HEAD_DIM), BF16),
        grid=(b, B_KV_HEADS, s // DENSE_TQ),
        in_specs=[
            pl.BlockSpec((None, DENSE_TQ, gw), lambda i, h, t: (i, t, qb_blk + h)),
            pl.BlockSpec((None, s, HEAD_DIM), lambda i, h, t: (i, 0, _KB0 + h)),
            pl.BlockSpec((None, s, HEAD_DIM), lambda i, h, t: (i, 0, _VB0 + h)),
            pl.BlockSpec((None, c, HEAD_DIM), lambda i, h, t: (i, 0, _KB0 + h)),
            pl.BlockSpec((None, c, HEAD_DIM), lambda i, h, t: (i, 0, _VB0 + h)),
        ],
        out_specs=pl.BlockSpec((None, DENSE_TQ, gw), lambda i, h, t: (i, t, h)),
        scratch_shapes=[pltpu.VMEM((rows, 1), F32), pltpu.VMEM((rows, 1), F32),
                        pltpu.VMEM((rows, HEAD_DIM), F32)],
        compiler_params=_cparams("arbitrary", "arbitrary", "arbitrary"),
        name="dense_attention",
    )(qkv, qkv, qkv, qkv_ctx, qkv_ctx)


def _layer_norm(r, g, b):
    mu = jnp.mean(r, axis=-1, keepdims=True)
    c = r - mu
    var = jnp.mean(c * c, axis=-1, keepdims=True)
    return c * lax.rsqrt(var + LN_EPS) * g + b


def _outproj_ln_kernel(n_in, *refs):
    a_refs = refs[:n_in]
    w_refs = refs[n_in:2 * n_in]
    x_ref, mod_ref, lng_ref, lnb_ref, rwh_ref, rwl_ref, xo_ref, h2_ref, lg_ref = refs[2 * n_in:]
    y = _dot(a_refs[0][...], w_refs[0][...])
    for a_ref, w_ref in zip(a_refs[1:], w_refs[1:]):
        y = y + _dot(a_ref[...], w_ref[...])
    r = DEEPNORM_ALPHA * x_ref[...] + mod_ref[2:3, :] * y
    xn = _layer_norm(r, lng_ref[...], lnb_ref[...])
    xo_ref[...] = xn
    h2 = xn * (1.0 + mod_ref[4:5, :]) + mod_ref[3:4, :]
    h2_ref[...] = h2
    hi, lo = _split_bf16(h2)
    rwh = rwh_ref[...]
    lg_ref[...] = _dot_nt(rwh, hi) + _dot_nt(rwh, lo) + _dot_nt(rwl_ref[...], hi)


def _outproj_ln(acts, weights, x2d, mod, ln_g, ln_b, router_wt, rows_per_mod):
    rows, d = x2d.shape
    tm = OUTPROJ_TM
    n_in = len(acts)
    n_e = router_wt.shape[0]
    tiles_per_mod = rows_per_mod // tm
    rwh, rwl = _split_bf16(router_wt)
    in_specs = (
        [pl.BlockSpec((tm, a.shape[1]), lambda i: (i, 0)) for a in acts]
        + [pl.BlockSpec(w.shape, lambda i: (0, 0), pipeline_mode=pl.Buffered(1)) for w in weights]
        + [
            pl.BlockSpec((tm, d), lambda i: (i, 0)),
            pl.BlockSpec((None, SUBLANES, d), lambda i: (i // tiles_per_mod, 0, 0)),
            pl.BlockSpec((1, d), lambda i: (0, 0)),
            pl.BlockSpec((1, d), lambda i: (0, 0)),
            pl.BlockSpec((n_e, d), lambda i: (0, 0)),
            pl.BlockSpec((n_e, d), lambda i: (0, 0)),
        ])
    return pl.pallas_call(
        functools.partial(_outproj_ln_kernel, n_in),
        out_shape=(jax.ShapeDtypeStruct((rows, d), F32), jax.ShapeDtypeStruct((rows, d), F32),
                   jax.ShapeDtypeStruct((n_e, rows), F32)),
        grid=(rows // tm,),
        in_specs=in_specs,
        out_specs=(pl.BlockSpec((tm, d), lambda i: (i, 0)), pl.BlockSpec((tm, d), lambda i: (i, 0)),
                   pl.BlockSpec((n_e, tm), lambda i: (0, i))),
        compiler_params=_cparams("arbitrary"),
        name="outproj_ln",
    )(*acts, *weights, x2d, mod, ln_g.reshape(1, d), ln_b.reshape(1, d), rwh, rwl)


def _routing_plan(logits_t, router_b):
    scores = jax.nn.sigmoid(logits_t.T)
    t = scores.shape[0]
    biased = scores + router_b.astype(F32)
    grp_score = lax.top_k(biased.reshape(t, N_GROUPS, -1), 2)[0].sum(-1)
    top_grp = lax.top_k(grp_score, TOP_GROUPS)[1]
    grp_mask = jax.nn.one_hot(top_grp, N_GROUPS, dtype=F32).sum(1) > 0
    masked = jnp.where(jnp.repeat(grp_mask, N_EXPERTS // N_GROUPS, axis=1), biased, -jnp.inf)
    top_e = lax.top_k(masked, TOP_K)[1]
    gw = jnp.take_along_axis(scores, top_e, axis=1)
    gw = gw / jnp.sum(gw, -1, keepdims=True) * ROUTED_SCALE
    tk = t * TOP_K
    flat_e = top_e.reshape(-1).astype(jnp.int32)
    flat_tok = jnp.repeat(jnp.arange(t, dtype=jnp.int32), TOP_K)
    order = jnp.argsort(flat_e, stable=True)
    e_s = flat_e[order]
    counts = jnp.bincount(flat_e, length=N_EXPERTS).astype(jnp.int32)
    starts = jnp.cumsum(counts) - counts
    pcounts = (counts + MOE_TM - 1) // MOE_TM * MOE_TM
    pends = jnp.cumsum(pcounts)
    pstarts = pends - pcounts
    dest = pstarts[e_s] + jnp.arange(tk, dtype=jnp.int32) - starts[e_s]
    n_blocks = tk // MOE_TM + N_EXPERTS
    n_rows = n_blocks * MOE_TM
    row_tok = jnp.zeros((n_rows,), jnp.int32).at[dest].set(flat_tok[order])
    pos = jnp.zeros((tk,), jnp.int32).at[order].set(dest)
    n_used = (pends[-1] // MOE_TM).astype(jnp.int32)
    blk_e = jnp.minimum(jnp.searchsorted(pends, jnp.arange(n_blocks, dtype=jnp.int32) * MOE_TM, side="right"),
                        N_EXPERTS - 1).astype(jnp.int32)
    last_e = blk_e[jnp.maximum(n_used - 1, 0)]
    blk_e = jnp.where(jnp.arange(n_blocks) < n_used, blk_e, last_e)
    tok_packed = row_tok[0::2] | (row_tok[1::2] << 16)
    return blk_e, n_used.reshape(1), tok_packed, pos, gw


def _moe_ffn_kernel(blk_e, n_used, tokp, h_hbm, wg_ref, wu_ref, wd_ref, y_ref,
                    xbuf, wgb, wub, wdb, sem):
    i = pl.program_id(0)
    tm = xbuf.shape[0]

    @pl.when(i < n_used[0])
    def _():
        def issue(r2, carry):
            word = tokp[i * (tm // 2) + r2]
            t0 = word & 0xFFFF
            t1 = lax.shift_right_logical(word, 16)
            pltpu.make_async_copy(h_hbm.at[pl.ds(t0, 1), :], xbuf.at[pl.ds(2 * r2, 1), :], sem).start()
            pltpu.make_async_copy(h_hbm.at[pl.ds(t1, 1), :], xbuf.at[pl.ds(2 * r2 + 1, 1), :], sem).start()
            return carry

        lax.fori_loop(0, tm // 2, issue, 0)

        @pl.when((i == 0) | (blk_e[i] != blk_e[jnp.maximum(i - 1, 0)]))
        def _():
            wgb[...] = wg_ref[...].astype(BF16)
            wub[...] = wu_ref[...].astype(BF16)
            wdb[...] = wd_ref[...].astype(BF16)

        pltpu.make_async_copy(h_hbm.at[pl.ds(0, tm), :], xbuf, sem).wait()
        x = xbuf[...].astype(BF16)
        hid = (_silu(_dot(x, wgb[...])) * _dot(x, wub[...])).astype(BF16)
        y_ref[...] = _dot(hid, wdb[...])

    @pl.when(i >= n_used[0])
    def _():
        y_ref[...] = jnp.zeros(y_ref.shape, y_ref.dtype)


def _moe_ffn(h2, blk_e, n_used, tok_packed, w_gate, w_up, w_down):
    t, d = h2.shape
    n_blocks = blk_e.shape[0]
    ff = w_gate.shape[2]
    tm = MOE_TM
    grid_spec = pltpu.PrefetchScalarGridSpec(
        num_scalar_prefetch=3,
        grid=(n_blocks,),
        in_specs=[
            pl.BlockSpec(memory_space=pl.ANY),
            pl.BlockSpec((None, d, ff), lambda i, be, nu, tp: (be[i], 0, 0)),
            pl.BlockSpec((None, d, ff), lambda i, be, nu, tp: (be[i], 0, 0)),
            pl.BlockSpec((None, ff, d), lambda i, be, nu, tp: (be[i], 0, 0)),
        ],
        out_specs=pl.BlockSpec((tm, d), lambda i, be, nu, tp: (i, 0)),
        scratch_shapes=[
            pltpu.VMEM((tm, d), F32),
            pltpu.VMEM((d, ff), BF16), pltpu.VMEM((d, ff), BF16), pltpu.VMEM((ff, d), BF16),
            pltpu.SemaphoreType.DMA,
        ],
    )
    return pl.pallas_call(
        _moe_ffn_kernel,
        out_shape=jax.ShapeDtypeStruct((n_blocks * tm, d), F32),
        grid_spec=grid_spec,
        compiler_params=_cparams("arbitrary"),
        name="moe_ffn",
    )(blk_e, n_used, tok_packed, h2, w_gate, w_up, w_down)


def _moe_combine_kernel(pos_ref, yb_hbm, gw_ref, h2_ref, x_ref, mod_ref, sg_ref, su_ref, sd_ref,
                        lng_ref, lnb_ref, o_ref, gbuf, sem):
    i = pl.program_id(0)
    tm = h2_ref.shape[0]

    def issue(r, carry):
        base = (i * tm + r) * TOP_K
        for k in range(TOP_K):
            pltpu.make_async_copy(yb_hbm.at[pl.ds(pos_ref[base + k], 1), :],
                                  gbuf.at[k, pl.ds(r, 1), :], sem).start()
        return carry

    lax.fori_loop(0, tm, issue, 0)

    h = h2_ref[...].astype(BF16)
    hid = (_silu(_dot(h, sg_ref[...])) * _dot(h, su_ref[...])).astype(BF16)
    out = _dot(hid, sd_ref[...])
    for k in range(TOP_K):
        pltpu.make_async_copy(yb_hbm.at[pl.ds(0, tm), :], gbuf.at[k], sem).wait()
    gw = gw_ref[...]
    for k in range(TOP_K):
        out = out + gw[:, k:k + 1] * gbuf[k]
    r = DEEPNORM_ALPHA * x_ref[...] + mod_ref[5:6, :] * out
    o_ref[...] = _layer_norm(r, lng_ref[...], lnb_ref[...])


def _moe_combine(yb, pos, gw, h2, x2d, mod, s_gate, s_up, s_down, ln_g, ln_b, rows_per_mod):
    t, d = h2.shape
    tm = COMB_TM
    sf = s_gate.shape[1]
    tiles_per_mod = rows_per_mod // tm
    grid_spec = pltpu.PrefetchScalarGridSpec(
        num_scalar_prefetch=1,
        grid=(t // tm,),
        in_specs=[
            pl.BlockSpec(memory_space=pl.ANY),
            pl.BlockSpec((tm, TOP_K), lambda i, p: (i, 0)),
            pl.BlockSpec((tm, d), lambda i, p: (i, 0)),
            pl.BlockSpec((tm, d), lambda i, p: (i, 0)),
            pl.BlockSpec((None, SUBLANES, d), lambda i, p: (i // tiles_per_mod, 0, 0)),
            pl.BlockSpec((d, sf), lambda i, p: (0, 0)),
            pl.BlockSpec((d, sf), lambda i, p: (0, 0)),
            pl.BlockSpec((sf, d), lambda i, p: (0, 0)),
            pl.BlockSpec((1, d), lambda i, p: (0, 0)),
            pl.BlockSpec((1, d), lambda i, p: (0, 0)),
        ],
        out_specs=pl.BlockSpec((tm, d), lambda i, p: (i, 0)),
        scratch_shapes=[pltpu.VMEM((TOP_K, tm, d), F32), pltpu.SemaphoreType.DMA],
    )
    return pl.pallas_call(
        _moe_combine_kernel,
        out_shape=jax.ShapeDtypeStruct((t, d), F32),
        grid_spec=grid_spec,
        compiler_params=_cparams("arbitrary"),
        name="moe_combine",
    )(pos, yb, gw, h2, x2d, mod, s_gate.astype(BF16), s_up.astype(BF16), s_down.astype(BF16),
      ln_g.reshape(1, d), ln_b.reshape(1, d))


def _moe_layer(x2d, h2, logits_t, mod, rows_per_mod, router_b, w_gate, w_up, w_down,
               s_gate, s_up, s_down, ln_g, ln_b):
    blk_e, n_used, tok_packed, pos, gw = _routing_plan(logits_t, router_b)
    yb = _moe_ffn(h2, blk_e, n_used, tok_packed, w_gate, w_up, w_down)
    return _moe_combine(yb, pos, gw, h2, x2d, mod, s_gate, s_up, s_down, ln_g, ln_b, rows_per_mod)


def _hy_inproj_kernel(x_ref, mod_ref, w_ref, o_ref):
    h = (x_ref[...] * (1.0 + mod_ref[1:2, :]) + mod_ref[0:1, :]).astype(BF16)
    o_ref[...] = _dot(h, w_ref[...])


def _hy_inproj(x2d, mod, w3_bf16, rows_per_mod):
    rows, d = x2d.shape
    n_out = w3_bf16.shape[0]
    tm = HY_TM
    tiles_per_mod = rows_per_mod // tm
    return pl.pallas_call(
        _hy_inproj_kernel,
        out_shape=jax.ShapeDtypeStruct((n_out, rows, d), F32),
        grid=(n_out, rows // tm),
        in_specs=[
            pl.BlockSpec((tm, d), lambda j, i: (i, 0)),
            pl.BlockSpec((None, SUBLANES, d), lambda j, i: (i // tiles_per_mod, 0, 0)),
            pl.BlockSpec((None, d, d), lambda j, i: (j, 0, 0)),
        ],
        out_specs=pl.BlockSpec((None, tm, d), lambda j, i: (j, i, 0)),
        compiler_params=_cparams("arbitrary", "arbitrary"),
        name="hyena_inproj",
    )(x2d, mod, w3_bf16)


def _short_conv_kernel(u_ref, w_ref, b_ref, o_ref):
    u = u_ref[...]
    seq = u.shape[0]
    t = lax.broadcasted_iota(jnp.int32, (seq, 1), 0)
    prev = jnp.where(t == 0, 0.0, pltpu.roll(u, 1, 0))
    nxt = jnp.where(t == seq - 1, 0.0, pltpu.roll(u, seq - 1, 0))
    o_ref[...] = prev * w_ref[0:1, :] + u * w_ref[1:2, :] + nxt * w_ref[2:3, :] + b_ref[...]


def _short_conv(u4, conv_w, conv_b):
    n_out, b, seq, d = u4.shape
    tc = CONV_TC
    return pl.pallas_call(
        _short_conv_kernel,
        out_shape=jax.ShapeDtypeStruct(u4.shape, F32),
        grid=(n_out, b, d // tc),
        in_specs=[
            pl.BlockSpec((None, None, seq, tc), lambda j, i, c: (j, i, 0, c)),
            pl.BlockSpec((None, SUBLANES, tc), lambda j, i, c: (j, 0, c)),
            pl.BlockSpec((None, 1, tc), lambda j, i, c: (j, 0, c)),
        ],
        out_specs=pl.BlockSpec((None, None, seq, tc), lambda j, i, c: (j, i, 0, c)),
        compiler_params=_cparams("arbitrary", "arbitrary", "arbitrary"),
        name="hyena_short_conv",
    )(u4, conv_w, conv_b)


def _filter_kernel(z_ref, t_ref, keep_ref, w1_ref, b1_ref, w2_ref, b2_ref, w3_ref, b3_ref, fq_ref,
                   w4_ref, dl_ref, o_ref):
    fq = fq_ref[...]
    a = jnp.sin(fq * (_dot3(z_ref[...], w1_ref[...]) + b1_ref[...]))
    a = jnp.sin(fq * (_dot3(a, w2_ref[...]) + b2_ref[...]))
    a = jnp.sin(fq * (_dot3(a, w3_ref[...]) + b3_ref[...]))
    hf = _dot3(a, w4_ref[...])
    o_ref[...] = hf * jnp.exp(-t_ref[...] * jnp.abs(dl_ref[...])) * keep_ref[...]


def _hyena_filter(seq, d, f_w1, f_b1, f_w2, f_b2, f_w3, f_b3, f_w4, f_freq):
    n = 2 * seq
    t = jnp.linspace(0.0, 1.0, seq, dtype=F32)[:, None]
    bands = (HY_EMB - 1) // 2
    w = (2.0 * math.pi / seq) * jnp.arange(seq, dtype=F32)
    fb = jnp.linspace(1e-4, bands - 1, bands, dtype=F32)
    fw = w[:, None] * fb[None, :]
    z = jnp.concatenate([t, jnp.cos(fw), -jnp.sin(fw)], axis=-1)
    m = jnp.arange(n)
    src = jnp.where(m < seq, m, jnp.clip(n - m, 0, seq - 1))
    wpad = HY_FILTER_WIDTH - HY_EMB
    z_full = jnp.pad(z[src], ((0, 0), (0, wpad)))
    t_full = t[src]
    keep = (m != seq).astype(F32)[:, None]
    w1p = jnp.pad(f_w1.astype(F32), ((0, wpad), (0, 0)))
    deltas = jnp.linspace(math.log(HY_DECAY_TARGET) / HY_DECAY_FAST,
                          math.log(HY_DECAY_TARGET) / HY_DECAY_SLOW, d, dtype=F32)[None, :]
    fw_ = HY_FILTER_WIDTH
    tiles_per_dir = seq // FILT_TM
    row = lambda v: v.astype(F32).reshape(1, -1)
    small = lambda shape: pl.BlockSpec(shape, lambda i, o: (0, 0))
    return pl.pallas_call(
        _filter_kernel,
        out_shape=jax.ShapeDtypeStruct((HY_ORDER, n, d), F32),
        grid=(n // FILT_TM, HY_ORDER),
        in_specs=[
            pl.BlockSpec((FILT_TM, fw_), lambda i, o: (i, 0)),
            pl.BlockSpec((FILT_TM, 1), lambda i, o: (i, 0)),
            pl.BlockSpec((FILT_TM, 1), lambda i, o: (i, 0)),
            small((fw_, fw_)), small((1, fw_)), small((fw_, fw_)), small((1, fw_)),
            small((fw_, fw_)), small((1, fw_)), small((1, fw_)),
            pl.BlockSpec((fw_, d), lambda i, o: (0, (i // tiles_per_dir) * HY_ORDER + o)),
            small((1, d)),
        ],
        out_specs=pl.BlockSpec((None, FILT_TM, d), lambda i, o: (o, i, 0)),
        compiler_params=_cparams("arbitrary", "arbitrary"),
        name="hyena_filter",
    )(z_full, t_full, keep, w1p, row(f_b1), f_w2.astype(F32), row(f_b2), f_w3.astype(F32), row(f_b3),
      row(f_freq), f_w4.astype(F32), deltas)


def _dft_tables():
    n = FFT_N1 * FFT_N2
    k1 = np.arange(FFT_K1)
    n1 = np.arange(FFT_N1)
    ang1 = 2.0 * np.pi * ((k1[:, None] * n1[None, :]) % FFT_N1) / FFT_N1
    f1 = np.zeros((2 * FFT_K1P, FFT_N1), np.float32)
    f1[:FFT_K1] = np.cos(ang1)
    f1[FFT_K1P:FFT_K1P + FFT_K1] = -np.sin(ang1)
    coef = np.where((k1 == 0) | (k1 == FFT_N1 // 2), 1.0, 2.0) / n
    g = np.zeros((FFT_N1 // 2, 2 * FFT_K1P), np.float32)
    ang1h = ang1[:, :FFT_N1 // 2].T
    g[:, :FFT_K1] = np.cos(ang1h) * coef
    g[:, FFT_K1P:FFT_K1P + FFT_K1] = -np.sin(ang1h) * coef
    n2 = np.arange(FFT_N2)
    k2 = np.arange(FFT_N2)
    mf = np.zeros((FFT_K1P, 2 * FFT_N2, 2 * FFT_N2), np.float32)
    mi = np.zeros((FFT_K1P, 2 * FFT_N2, 2 * FFT_N2), np.float32)
    for a in range(FFT_K1):
        k = a + FFT_N1 * k2
        ang = 2.0 * np.pi * ((k[:, None] * n2[None, :]) % n) / n
        er, ei = np.cos(ang), -np.sin(ang)
        mf[a] = np.block([[er, -ei], [ei, er]])
        mi[a] = np.block([[er.T, ei.T], [-ei.T, er.T]])
    return f1, g, mf, mi


def _dft_stage1_kernel(x_ref, f_ref, re_ref, im_ref):
    y = _dot(f_ref[...], x_ref[...].astype(BF16))
    re_ref[...] = y[:FFT_K1P]
    im_ref[...] = y[FFT_K1P:]


def _dft_stage1(x3, f1):
    b, n1, lanes = x3.shape
    spec_o = pl.BlockSpec((None, FFT_K1P, FFT_TL), lambda i, j: (i, 0, j))
    return pl.pallas_call(
        _dft_stage1_kernel,
        out_shape=(jax.ShapeDtypeStruct((b, FFT_K1P, lanes), F32),) * 2,
        grid=(b, lanes // FFT_TL),
        in_specs=[pl.BlockSpec((None, n1, FFT_TL), lambda i, j: (i, 0, j)),
                  pl.BlockSpec(f1.shape, lambda i, j: (0, 0))],
        out_specs=(spec_o, spec_o),
        compiler_params=_cparams("arbitrary", "arbitrary"),
        name="dft_stage1",
    )(x3, f1)


def _dft_stage2_kernel(ar_ref, ai_ref, mf_ref, xr_ref, xi_ref):
    a = jnp.concatenate([ar_ref[...], ai_ref[...]], axis=0).astype(BF16)
    x = _dot(mf_ref[...], a)
    xr_ref[...] = x[:FFT_N2]
    xi_ref[...] = x[FFT_N2:]


def _dft_stage2(ar, ai, mf):
    b, _, n2, d = ar.shape
    spec = pl.BlockSpec((None, None, n2, d), lambda k, i: (i, k, 0, 0))
    return pl.pallas_call(
        _dft_stage2_kernel,
        out_shape=(jax.ShapeDtypeStruct(ar.shape, F32),) * 2,
        grid=(FFT_K1P, b),
        in_specs=[spec, spec, pl.BlockSpec((None, 2 * n2, 2 * n2), lambda k, i: (k, 0, 0))],
        out_specs=(spec, spec),
        compiler_params=_cparams("arbitrary", "arbitrary"),
        name="dft_stage2",
    )(ar, ai, mf)


def _spectral_kernel(ar_ref, ai_ref, hr_ref, hi_ref, mf_ref, mi_ref, br_ref, bi_ref):
    a = jnp.concatenate([ar_ref[...], ai_ref[...]], axis=0).astype(BF16)
    x = _dot(mf_ref[...], a)
    xr, xi = x[:FFT_N2], x[FFT_N2:]
    hr, hi = hr_ref[...], hi_ref[...]
    y = jnp.concatenate([xr * hr - xi * hi, xr * hi + xi * hr], axis=0).astype(BF16)
    bv = _dot(mi_ref[...], y)
    br_ref[...] = bv[:FFT_N2]
    bi_ref[...] = bv[FFT_N2:]


def _spectral_multiply(ar, ai, hr, hi, order, mf, mi):
    b, _, n2, d = ar.shape
    spec = pl.BlockSpec((None, None, n2, d), lambda k, i: (i, k, 0, 0))
    spec_h = pl.BlockSpec((None, None, n2, d), lambda k, i: (order, k, 0, 0))
    spec_m = pl.BlockSpec((None, 2 * n2, 2 * n2), lambda k, i: (k, 0, 0))
    return pl.pallas_call(
        _spectral_kernel,
        out_shape=(jax.ShapeDtypeStruct(ar.shape, F32),) * 2,
        grid=(FFT_K1P, b),
        in_specs=[spec, spec, spec_h, spec_h, spec_m, spec_m],
        out_specs=(spec, spec),
        compiler_params=_cparams("arbitrary", "arbitrary"),
        name="dft_spectral",
    )(ar, ai, hr, hi, mf, mi)


def _idft_gate_kernel(br_ref, bi_ref, g_ref, z_ref, gate_ref, bias_ref, o_ref):
    g = g_ref[...]
    y = _dot(g[:, :FFT_K1P], br_ref[...].astype(BF16)) + _dot(g[:, FFT_K1P:], bi_ref[...].astype(BF16))
    o_ref[...] = gate_ref[...] * (y + z_ref[...] * bias_ref[...])


def _idft_gate(br, bi, g, z3, gate3, bias_t):
    b, n1h, lanes = z3.shape
    spec_b = pl.BlockSpec((None, FFT_K1P, FFT_TL), lambda i, j: (i, 0, j))
    spec_x = pl.BlockSpec((None, n1h, FFT_TL), lambda i, j: (i, 0, j))
    return pl.pallas_call(
        _idft_gate_kernel,
        out_shape=jax.ShapeDtypeStruct(z3.shape, F32),
        grid=(b, lanes // FFT_TL),
        in_specs=[spec_b, spec_b, pl.BlockSpec(g.shape, lambda i, j: (0, 0)), spec_x, spec_x,
                  pl.BlockSpec((1, FFT_TL), lambda i, j: (0, j))],
        out_specs=spec_x,
        compiler_params=_cparams("arbitrary", "arbitrary"),
        name="idft_gate",
    )(br, bi, g, z3, gate3, bias_t)


def _hyena_mixer(x2d, mod, batch, seq, w_in, conv_w, conv_b, f_w1, f_b1, f_w2, f_b2, f_w3, f_b3, f_w4,
                 f_freq, bias_d):
    d = x2d.shape[1]
    assert 2 * seq == FFT_N1 * FFT_N2
    w3 = w_in.reshape(d, 3, d).transpose(1, 0, 2).astype(BF16)
    u = _hy_inproj(x2d, mod, w3, seq)
    cw = jnp.pad(conv_w.reshape(3, 3, d).transpose(1, 0, 2), ((0, 0), (0, SUBLANES - 3), (0, 0)))
    cb = conv_b.reshape(3, 1, d)
    u = _short_conv(u.reshape(3, batch, seq, d), cw, cb)
    f1, g, mf, mi = _dft_tables()
    f1 = jnp.asarray(f1).astype(BF16)
    g = jnp.asarray(g).astype(BF16)
    mf = jnp.asarray(mf).astype(BF16)
    mi = jnp.asarray(mi).astype(BF16)
    lanes = FFT_N2 * d
    kern = _hyena_filter(seq, d, f_w1, f_b1, f_w2, f_b2, f_w3, f_b3, f_w4, f_freq)
    kr, ki = _dft_stage1(kern.reshape(HY_ORDER, FFT_N1, lanes), f1)
    shape4 = lambda a: a.reshape(a.shape[0], FFT_K1P, FFT_N2, d)
    hr, hi = _dft_stage2(shape4(kr), shape4(ki), mf)
    z = u[0].reshape(batch, FFT_N1 // 2, lanes)
    f1h = f1[:, :FFT_N1 // 2]
    for o in range(HY_ORDER):
        ar, ai = _dft_stage1(z, f1h)
        br, bi = _spectral_multiply(shape4(ar), shape4(ai), hr, hi, o, mf, mi)
        bias_t = jnp.tile(bias_d[o].astype(F32), FFT_N2).reshape(1, lanes)
        z = _idft_gate(br.reshape(batch, FFT_K1P, lanes), bi.reshape(batch, FFT_K1P, lanes), g, z,
                       u[1 + o].reshape(batch, FFT_N1 // 2, lanes), bias_t)
    return z.reshape(batch * seq, d)


def _mod_table(m_layer, d):
    m6 = m_layer.reshape(m_layer.shape[0], 6, d)
    return jnp.pad(m6, ((0, 0), (0, SUBLANES - 6), (0, 0)))


def kernel(x, c, ctx, c_ctx, ada_w, ada_b, ln_g, ln_b, attn_w_in, attn_w_out, attn_sink, attn_q_gain, attn_k_gain, hy_w_in, hy_conv_w, hy_conv_b, hy_f_w1, hy_f_b1, hy_f_w2, hy_f_b2, hy_f_w3, hy_f_b3, hy_f_w4, hy_f_freq, hy_bias_d, hy_w_out, moe_router_w, moe_router_b, moe_w_gate, moe_w_up, moe_w_down, moe_s_gate, moe_s_up, moe_s_down):
    batch, seq, d = x.shape
    n_ctx = ctx.shape[1]
    depth = ada_w.shape[0]
    assert batch + 1 <= SUBLANES
    cc = jnp.concatenate([c, c_ctx[None, :], jnp.zeros((SUBLANES - batch - 1, d), c.dtype)], axis=0)
    m_all = _ada_modulation(cc.astype(F32), ada_w, ada_b)
    x2d = x.reshape(batch * seq, d)
    cos_t, sin_t = _rope_tables(seq)
    for l in range(depth):
        mod = _mod_table(m_all[l], d)
        i = l // 2
        router_wt = moe_router_w[l].astype(F32).T
        if l % 2 == 0:
            w_in = attn_w_in[i].astype(BF16)
            qkv = _attn_inproj(x2d, mod, w_in, cos_t, sin_t, attn_q_gain[i], attn_k_gain[i],
                               seq, 0, INPROJ_TM)
            ones = jnp.ones((n_ctx, HEAD_DIM), F32)
            qkv_ctx = _attn_inproj(ctx.reshape(batch * n_ctx, d), mod, w_in, ones, jnp.zeros_like(ones),
                                   attn_q_gain[i], attn_k_gain[i], batch * n_ctx, batch, n_ctx)
            qkv = qkv.reshape(batch, seq, -1)
            qkv_ctx = qkv_ctx.reshape(batch, n_ctx, -1)
            o_a = _win_attention(qkv, qkv_ctx, attn_sink[i].astype(F32))
            o_b = _dense_attention(qkv, qkv_ctx)
            w_out = attn_w_out[i].astype(BF16)
            na = A_Q_HEADS * HEAD_DIM
            acts = [o_a.reshape(batch * seq, na), o_b.reshape(batch * seq, -1)]
            weights = [w_out[:na], w_out[na:]]
        else:
            z = _hyena_mixer(x2d, mod, batch, seq, hy_w_in[i], hy_conv_w[i], hy_conv_b[i], hy_f_w1[i],
                             hy_f_b1[i], hy_f_w2[i], hy_f_b2[i], hy_f_w3[i], hy_f_b3[i], hy_f_w4[i],
                             hy_f_freq[i], hy_bias_d[i])
            acts = [z.astype(BF16)]
            weights = [hy_w_out[i].astype(BF16)]
        x2d, h2, logits_t = _outproj_ln(acts, weights, x2d, mod, ln_g[l, 0], ln_b[l, 0], router_wt, seq)
        x2d = _moe_layer(x2d, h2, logits_t, mod, seq, moe_router_b[l], moe_w_gate[l], moe_w_up[l],
                         moe_w_down[l], moe_s_gate[l], moe_s_up[l], moe_s_down[l], ln_g[l, 1], ln_b[l, 1])
    return x2d.reshape(batch, seq, d)
```

```python
import functools
import math

import numpy as np
import jax
import jax.numpy as jnp
from jax import lax
from jax.experimental import pallas as pl
from jax.experimental.pallas import tpu as pltpu

F32 = jnp.float32
BF16 = jnp.bfloat16

DEPTH = 2
GRID_W = 64
HEAD_DIM = 128
A_Q_HEADS, A_KV_HEADS, B_Q_HEADS, B_KV_HEADS = 8, 2, 8, 2
GQA_GROUP = A_Q_HEADS // A_KV_HEADS
WINDOW = 128
Q_BLOCK = 128
ROPE_THETA = 10000.0
QK_NORM_EPS = 1e-6
LN_EPS = 1e-5
HY_ORDER = 2
HY_EMB = 33
HY_FILTER_WIDTH = 64
HY_DECAY_FAST, HY_DECAY_SLOW, HY_DECAY_TARGET = 0.3, 1.5, 1e-2
N_EXPERTS = 64
TOP_K = 8
N_GROUPS = 8
TOP_GROUPS = 4
ROUTED_SCALE = 2.5
DEEPNORM_ALPHA = (2 * DEPTH) ** 0.25

LANES = 128
SUBLANES = 8
VMEM_LIMIT = 56 * 1024 * 1024

ADA_TN = 1024
INPROJ_TM = 512
DENSE_TQ = 256
DENSE_TK = 512
OUTPROJ_TM = 256
HY_TM = 512
CONV_TC = 512
MOE_TM = 256
ROUTE_TT = 512
DISP_TM = 256
COMB_TM = 256
FFT_N1, FFT_N2 = 64, 128
FFT_K1 = FFT_N1 // 2 + 1
FFT_K1P = 40
FILT_TM = 512


def _cparams(*sem):
    return pltpu.CompilerParams(dimension_semantics=sem, vmem_limit_bytes=VMEM_LIMIT)


def _silu(v):
    return v * jax.nn.sigmoid(v)


def _split_bf16(v):
    hi = v.astype(BF16)
    lo = (v - hi.astype(F32)).astype(BF16)
    return hi, lo


def _dot(a, b):
    return jnp.dot(a, b, preferred_element_type=F32)


def _dot_nt(a, b):
    return lax.dot_general(a, b, (((1,), (1,)), ((), ())), preferred_element_type=F32)


def _pack_bf16_pairs(v):
    n = v.shape[1] // 2
    bits = pltpu.bitcast(v.astype(BF16).astype(F32), jnp.uint32)
    return (bits[:, :n] >> 16) | (bits[:, n:] & jnp.uint32(0xFFFF0000))


def _unpack_bf16_pairs(p):
    lo = pltpu.bitcast(p << 16, F32)
    hi = pltpu.bitcast(p & jnp.uint32(0xFFFF0000), F32)
    return lo, hi


def _dot3(a, b):
    ah, al = _split_bf16(a)
    bh, bl = _split_bf16(b)
    return _dot(ah, bh) + _dot(al, bh) + _dot(ah, bl)


def _ada_kernel(c_ref, w_ref, b_ref, o_ref):
    s = _silu(c_ref[...]).astype(BF16)
    o_ref[...] = _dot(s, w_ref[...].astype(BF16)) + b_ref[...]


def _ada_modulation(cc, ada_w, ada_b):
    depth, d, n = ada_w.shape
    rows = cc.shape[0]
    return pl.pallas_call(
        _ada_kernel,
        out_shape=jax.ShapeDtypeStruct((depth, rows, n), F32),
        grid=(depth, n // ADA_TN),
        in_specs=[
            pl.BlockSpec((rows, d), lambda l, j: (0, 0)),
            pl.BlockSpec((None, d, ADA_TN), lambda l, j: (l, 0, j)),
            pl.BlockSpec((None, 1, ADA_TN), lambda l, j: (l, 0, j)),
        ],
        out_specs=pl.BlockSpec((None, rows, ADA_TN), lambda l, j: (l, 0, j)),
        compiler_params=_cparams("arbitrary", "arbitrary"),
        name="ada_modulation",
    )(cc, ada_w, ada_b.reshape(depth, 1, n))


_N_SLOTS = (A_Q_HEADS + 2 * A_KV_HEADS + B_Q_HEADS + 2 * B_KV_HEADS)
_QA0, _KA0, _VA0 = 0, A_Q_HEADS, A_Q_HEADS + A_KV_HEADS
_QB0 = A_Q_HEADS + 2 * A_KV_HEADS
_KB0, _VB0 = _QB0 + B_Q_HEADS, _QB0 + B_Q_HEADS + B_KV_HEADS


def _slot_kind(j):
    if j < _KA0:
        return "qa"
    if j < _VA0:
        return "ka"
    if j < _QB0:
        return "v"
    if j < _KB0:
        return "qb"
    if j < _VB0:
        return "kb"
    return "v"


def _attn_inproj_kernel(x_ref, mod_ref, w_ref, cos_ref, sin_ref, qg_ref, kg_ref, o_ref):
    x = x_ref[...]
    h = (x * (1.0 + mod_ref[1:2, :]) + mod_ref[0:1, :]).astype(BF16)
    y = _dot(h, w_ref[...])
    cos = cos_ref[...]
    sin = sin_ref[...]
    lane = lax.broadcasted_iota(jnp.int32, cos.shape, 1)
    low_half = (lane & (HEAD_DIM // 2 - 1)) < (HEAD_DIM // 4)
    q_scale = HEAD_DIM ** -0.5
    for j in range(_N_SLOTS):
        kind = _slot_kind(j)
        blk = y[:, j * HEAD_DIM:(j + 1) * HEAD_DIM]
        if kind in ("qb", "kb"):
            gain = qg_ref[...] if kind == "qb" else kg_ref[...]
            ms = jnp.mean(blk * blk, axis=-1, keepdims=True)
            blk = blk * lax.rsqrt(ms + QK_NORM_EPS) * gain
        if kind != "v":
            partner = jnp.where(low_half,
                                pltpu.roll(blk, HEAD_DIM - HEAD_DIM // 4, 1),
                                pltpu.roll(blk, HEAD_DIM // 4, 1))
            blk = blk * cos + partner * sin
        if kind in ("qa", "qb"):
            blk = blk * q_scale
        o_ref[:, j * HEAD_DIM:(j + 1) * HEAD_DIM] = blk.astype(o_ref.dtype)


def _attn_inproj(x2d, mod, w_bf16, cos_t, sin_t, q_gain, k_gain, rows_per_mod, mod_base, tm):
    rows, d = x2d.shape
    n = w_bf16.shape[1]
    tiles_per_mod = rows_per_mod // tm
    tab_tiles = cos_t.shape[0] // tm
    return pl.pallas_call(
        _attn_inproj_kernel,
        out_shape=jax.ShapeDtypeStruct((rows, n), BF16),
        grid=(rows // tm,),
        in_specs=[
            pl.BlockSpec((tm, d), lambda i: (i, 0)),
            pl.BlockSpec((None, SUBLANES, d), lambda i: (mod_base + i // tiles_per_mod, 0, 0)),
            pl.BlockSpec((d, n), lambda i: (0, 0), pipeline_mode=pl.Buffered(1)),
            pl.BlockSpec((tm, HEAD_DIM), lambda i: (i % tab_tiles, 0)),
            pl.BlockSpec((tm, HEAD_DIM), lambda i: (i % tab_tiles, 0)),
            pl.BlockSpec((1, HEAD_DIM), lambda i: (0, 0)),
            pl.BlockSpec((1, HEAD_DIM), lambda i: (0, 0)),
        ],
        out_specs=pl.BlockSpec((tm, n), lambda i: (i, 0)),
        compiler_params=_cparams("arbitrary"),
        name="attn_inproj",
    )(x2d, mod, w_bf16, cos_t, sin_t, q_gain.reshape(1, -1), k_gain.reshape(1, -1))


def _rope_tables(seq):
    rows = seq // GRID_W
    row = jnp.repeat(jnp.arange(rows, dtype=F32), GRID_W)
    col = jnp.tile(jnp.arange(GRID_W, dtype=F32), rows)
    n_freq = HEAD_DIM // 4
    inv = ROPE_THETA ** (-jnp.arange(n_freq, dtype=F32) / n_freq)
    ang_r, ang_c = row[:, None] * inv, col[:, None] * inv
    cos = jnp.concatenate([jnp.cos(ang_r)] * 2 + [jnp.cos(ang_c)] * 2, axis=-1)
    sin = jnp.concatenate([-jnp.sin(ang_r), jnp.sin(ang_r), -jnp.sin(ang_c), jnp.sin(ang_c)], axis=-1)
    return cos, sin


def _win_attn_kernel(sink_ref, q_ref, k_ref, v_ref, kc_ref, vc_ref, o_ref):
    seq = q_ref.shape[0]
    hk = pl.program_id(1)
    n_loc = 3 * Q_BLOCK
    rows = GQA_GROUP * Q_BLOCK
    row = lax.broadcasted_iota(jnp.int32, (rows, 1), 0)
    col = lax.broadcasted_iota(jnp.int32, (1, n_loc), 1)
    sink_col = jnp.full((rows, 1), sink_ref[hk * GQA_GROUP + GQA_GROUP - 1], F32)
    for g in range(GQA_GROUP - 2, -1, -1):
        sink_col = jnp.where(row < (g + 1) * Q_BLOCK, sink_ref[hk * GQA_GROUP + g], sink_col)
    kc = kc_ref[...]
    vc = vc_ref[...]

    def body(n, carry):
        q0 = pl.multiple_of(n * Q_BLOCK, Q_BLOCK)
        start = pl.multiple_of(jnp.clip(q0 - Q_BLOCK, 0, seq - n_loc), Q_BLOCK)
        q = jnp.concatenate(
            [q_ref[pl.ds(q0, Q_BLOCK), g * HEAD_DIM:(g + 1) * HEAD_DIM] for g in range(GQA_GROUP)], axis=0)
        kw = k_ref[pl.ds(start, n_loc), :]
        vw = v_ref[pl.ds(start, n_loc), :]
        s_loc = _dot_nt(q, kw)
        s_ctx = _dot_nt(q, kc)
        qpos = q0 + (row & (Q_BLOCK - 1))
        kpos = start + col
        s_loc = jnp.where(jnp.abs(qpos - kpos) <= WINDOW, s_loc, -jnp.inf)
        m = jnp.maximum(jnp.maximum(jnp.max(s_loc, -1, keepdims=True), jnp.max(s_ctx, -1, keepdims=True)),
                        sink_col)
        p_loc = jnp.exp(s_loc - m)
        p_ctx = jnp.exp(s_ctx - m)
        denom = (jnp.sum(p_loc, -1, keepdims=True) + jnp.sum(p_ctx, -1, keepdims=True)
                 + jnp.exp(sink_col - m))
        o = (_dot(p_loc.astype(BF16), vw) + _dot(p_ctx.astype(BF16), vc)) / denom
        for g in range(GQA_GROUP):
            o_ref[pl.ds(q0, Q_BLOCK), g * HEAD_DIM:(g + 1) * HEAD_DIM] = (
                o[g * Q_BLOCK:(g + 1) * Q_BLOCK].astype(o_ref.dtype))
        return carry

    lax.fori_loop(0, seq // Q_BLOCK, body, 0)


def _win_attention(qkv, qkv_ctx, sink):
    b, s, _ = qkv.shape
    c = qkv_ctx.shape[1]
    gw = GQA_GROUP * HEAD_DIM
    return pl.pallas_call(
        _win_attn_kernel,
        out_shape=jax.ShapeDtypeStruct((b, s, A_Q_HEADS * HEAD_DIM), BF16),
        grid=(b, A_KV_HEADS),
        in_specs=[
            pl.BlockSpec(memory_space=pltpu.SMEM),
            pl.BlockSpec((None, s, gw), lambda i, h: (i, 0, h)),
            pl.BlockSpec((None, s, HEAD_DIM), lambda i, h: (i, 0, _KA0 + h)),
            pl.BlockSpec((None, s, HEAD_DIM), lambda i, h: (i, 0, _VA0 + h)),
            pl.BlockSpec((None, c, HEAD_DIM), lambda i, h: (i, 0, _KA0 + h)),
            pl.BlockSpec((None, c, HEAD_DIM), lambda i, h: (i, 0, _VA0 + h)),
        ],
        out_specs=pl.BlockSpec((None, s, gw), lambda i, h: (i, 0, h)),
        compiler_params=_cparams("arbitrary", "arbitrary"),
        name="win_attention",
    )(sink, qkv, qkv, qkv, qkv_ctx, qkv_ctx)


def _dense_attn_kernel(q_ref, k_ref, v_ref, kc_ref, vc_ref, o_ref, m_sc, l_sc, acc_sc):
    seq = k_ref.shape[0]
    m_sc[...] = jnp.full(m_sc.shape, -jnp.inf, F32)
    l_sc[...] = jnp.zeros(l_sc.shape, F32)
    acc_sc[...] = jnp.zeros(acc_sc.shape, F32)

    def step(kb, vb):
        reps = kb.shape[0] // LANES
        for g in range(GQA_GROUP):
            s = _dot_nt(q_ref[:, g * HEAD_DIM:(g + 1) * HEAD_DIM], kb)
            m_prev = m_sc[g]
            m_new = jnp.maximum(m_prev, jnp.max(s, -1, keepdims=True))
            alpha = jnp.exp(m_prev - m_new)
            p = jnp.exp(s - jnp.concatenate([m_new] * reps, axis=1))
            l_sc[g] = alpha * l_sc[g] + jnp.sum(p, -1, keepdims=True)
            acc_sc[g] = alpha * acc_sc[g] + _dot(p.astype(BF16), vb)
            m_sc[g] = m_new

    def body(j, carry):
        k0 = pl.multiple_of(j * DENSE_TK, DENSE_TK)
        step(k_ref[pl.ds(k0, DENSE_TK), :], v_ref[pl.ds(k0, DENSE_TK), :])
        return carry

    lax.fori_loop(0, seq // DENSE_TK, body, 0)
    step(kc_ref[...], vc_ref[...])
    for g in range(GQA_GROUP):
        o_ref[:, g * HEAD_DIM:(g + 1) * HEAD_DIM] = (acc_sc[g] / l_sc[g]).astype(o_ref.dtype)


def _dense_attention(qkv, qkv_ctx):
    b, s, _ = qkv.shape
    c = qkv_ctx.shape[1]
    gw = GQA_GROUP * HEAD_DIM
    qb_blk = _QB0 * HEAD_DIM // gw
    stat = pltpu.VMEM((GQA_GROUP, DENSE_TQ, LANES), F32)
    return pl.pallas_call(
        _dense_attn_kernel,
        out_shape=jax.ShapeDtypeStruct((b, s, B_Q_HEADS * HEAD_DIM), BF16),
        grid=(b, B_KV_HEADS, s // DENSE_TQ),
        in_specs=[
            pl.BlockSpec((None, DENSE_TQ, gw), lambda i, h, t: (i, t, qb_blk + h)),
            pl.BlockSpec((None, s, HEAD_DIM), lambda i, h, t: (i, 0, _KB0 + h)),
            pl.BlockSpec((None, s, HEAD_DIM), lambda i, h, t: (i, 0, _VB0 + h)),
            pl.BlockSpec((None, c, HEAD_DIM), lambda i, h, t: (i, 0, _KB0 + h)),
            pl.BlockSpec((None, c, HEAD_DIM), lambda i, h, t: (i, 0, _VB0 + h)),
        ],
        out_specs=pl.BlockSpec((None, DENSE_TQ, gw), lambda i, h, t: (i, t, h)),
        scratch_shapes=[stat, stat, stat],
        compiler_params=_cparams("arbitrary", "arbitrary", "arbitrary"),
        name="dense_attention",
    )(qkv, qkv, qkv, qkv_ctx, qkv_ctx)


def _layer_norm(r, g, b):
    mu = jnp.mean(r, axis=-1, keepdims=True)
    c = r - mu
    var = jnp.mean(c * c, axis=-1, keepdims=True)
    return c * lax.rsqrt(var + LN_EPS) * g + b


def _outproj_ln_kernel(n_in, *refs):
    a_refs = refs[:n_in]
    w_refs = refs[n_in:2 * n_in]
    x_ref, mod_ref, lng_ref, lnb_ref, rwh_ref, rwl_ref, xo_ref, h2_ref, lg_ref = refs[2 * n_in:]
    y = _dot(a_refs[0][...].astype(BF16), w_refs[0][...])
    for a_ref, w_ref in zip(a_refs[1:], w_refs[1:]):
        y = y + _dot(a_ref[...].astype(BF16), w_ref[...])
    r = DEEPNORM_ALPHA * x_ref[...] + mod_ref[2:3, :] * y
    xn = _layer_norm(r, lng_ref[...], lnb_ref[...])
    xo_ref[...] = xn
    h2 = xn * (1.0 + mod_ref[4:5, :]) + mod_ref[3:4, :]
    h2_ref[...] = _pack_bf16_pairs(h2)
    hi, lo = _split_bf16(h2)
    rwh = rwh_ref[...]
    lg_ref[...] = _dot_nt(rwh, hi) + _dot_nt(rwh, lo) + _dot_nt(rwl_ref[...], hi)


def _outproj_ln(acts, weights, x2d, mod, ln_g, ln_b, router_wt, rows_per_mod):
    rows, d = x2d.shape
    tm = OUTPROJ_TM
    n_in = len(acts)
    n_e = router_wt.shape[0]
    tiles_per_mod = rows_per_mod // tm
    rwh, rwl = _split_bf16(router_wt)
    in_specs = (
        [pl.BlockSpec((tm, a.shape[1]), lambda i: (i, 0)) for a in acts]
        + [pl.BlockSpec(w.shape, lambda i: (0, 0), pipeline_mode=pl.Buffered(1)) for w in weights]
        + [
            pl.BlockSpec((tm, d), lambda i: (i, 0)),
            pl.BlockSpec((None, SUBLANES, d), lambda i: (i // tiles_per_mod, 0, 0)),
            pl.BlockSpec((1, d), lambda i: (0, 0)),
            pl.BlockSpec((1, d), lambda i: (0, 0)),
            pl.BlockSpec((n_e, d), lambda i: (0, 0)),
            pl.BlockSpec((n_e, d), lambda i: (0, 0)),
        ])
    return pl.pallas_call(
        functools.partial(_outproj_ln_kernel, n_in),
        out_shape=(jax.ShapeDtypeStruct((rows, d), F32), jax.ShapeDtypeStruct((rows, d // 2), jnp.uint32),
                   jax.ShapeDtypeStruct((n_e, rows), F32)),
        grid=(rows // tm,),
        in_specs=in_specs,
        out_specs=(pl.BlockSpec((tm, d), lambda i: (i, 0)), pl.BlockSpec((tm, d // 2), lambda i: (i, 0)),
                   pl.BlockSpec((n_e, tm), lambda i: (0, i))),
        compiler_params=_cparams("arbitrary"),
        name="outproj_ln",
    )(*acts, *weights, x2d, mod, ln_g.reshape(1, d), ln_b.reshape(1, d), rwh, rwl)


def _route_kernel(lg_ref, b_ref, tri_ref, e_ref, w_ref, r_ref, cnt_ref, carry_sc):
    @pl.when(pl.program_id(0) == 0)
    def _():
        carry_sc[...] = jnp.zeros(carry_sc.shape, F32)

    tt = lg_ref.shape[1]
    gsz = N_EXPERTS // N_GROUPS
    s = jax.nn.sigmoid(lg_ref[...])
    biased = s + b_ref[...]
    sub = lax.broadcasted_iota(jnp.int32, (gsz, tt), 0).astype(F32)
    rows = []
    for g in range(N_GROUPS):
        v = biased[g * gsz:(g + 1) * gsz]
        m1 = jnp.max(v, axis=0, keepdims=True)
        first = jnp.min(jnp.where(v == m1, sub, float(gsz)), axis=0, keepdims=True)
        m2 = jnp.max(jnp.where(sub == first, -jnp.inf, v), axis=0, keepdims=True)
        rows.append(m1 + m2)
    gs = jnp.concatenate(rows, axis=0)
    gidx = lax.broadcasted_iota(jnp.int32, (N_GROUPS, tt), 0)
    grank = jnp.zeros((N_GROUPS, tt), F32)
    for g2 in range(N_GROUPS):
        row = gs[g2:g2 + 1]
        ahead = jnp.where(row > gs, 1.0, jnp.where(row == gs, jnp.where(gidx > g2, 1.0, 0.0), 0.0))
        grank = grank + ahead
    gkeep = jnp.where(grank < TOP_GROUPS, 1.0, 0.0)
    keep = jnp.concatenate([jnp.broadcast_to(gkeep[g:g + 1], (gsz, tt)) for g in range(N_GROUPS)], axis=0)
    cur = jnp.where(keep > 0.5, biased, -jnp.inf)
    eidx = lax.broadcasted_iota(jnp.int32, (N_EXPERTS, tt), 0).astype(F32)
    chosen = jnp.zeros((N_EXPERTS, tt), F32)
    e_rows, s_rows = [], []
    for _ in range(TOP_K):
        m = jnp.max(cur, axis=0, keepdims=True)
        idx = jnp.min(jnp.where(cur == m, eidx, float(N_EXPERTS)), axis=0, keepdims=True)
        hit = eidx == idx
        s_rows.append(jnp.sum(jnp.where(hit, s, 0.0), axis=0, keepdims=True))
        e_rows.append(idx)
        chosen = jnp.where(hit, 1.0, chosen)
        cur = jnp.where(hit, -jnp.inf, cur)
    top_e = jnp.concatenate(e_rows, axis=0)
    sc = jnp.concatenate(s_rows, axis=0)
    w_ref[...] = sc / jnp.sum(sc, axis=0, keepdims=True) * ROUTED_SCALE
    e_ref[...] = top_e.astype(jnp.int32)
    carry = carry_sc[...]
    cnt = _dot(chosen.astype(BF16), tri_ref[...]) + jnp.concatenate([carry] * (tt // LANES), axis=1)
    rank = jnp.zeros((TOP_K, tt), F32)
    for e in range(N_EXPERTS):
        rank = rank + jnp.where(top_e == float(e), cnt[e:e + 1], 0.0)
    r_ref[...] = rank.astype(jnp.int32)
    carry = carry + jnp.sum(chosen, axis=1, keepdims=True)
    carry_sc[...] = carry
    cnt_ref[...] = carry


def _route(logits_t, router_b):
    n_e, t = logits_t.shape
    tt = ROUTE_TT
    tri = (jnp.arange(tt)[:, None] < jnp.arange(tt)[None, :]).astype(BF16)
    tok_spec = pl.BlockSpec((TOP_K, tt), lambda i: (0, i))
    return pl.pallas_call(
        _route_kernel,
        out_shape=(jax.ShapeDtypeStruct((TOP_K, t), jnp.int32), jax.ShapeDtypeStruct((TOP_K, t), F32),
                   jax.ShapeDtypeStruct((TOP_K, t), jnp.int32), jax.ShapeDtypeStruct((n_e, LANES), F32)),
        grid=(t // tt,),
        in_specs=[pl.BlockSpec((n_e, tt), lambda i: (0, i)),
                  pl.BlockSpec((n_e, 1), lambda i: (0, 0)),
                  pl.BlockSpec((tt, tt), lambda i: (0, 0))],
        out_specs=(tok_spec, tok_spec, tok_spec, pl.BlockSpec((n_e, LANES), lambda i: (0, 0))),
        scratch_shapes=[pltpu.VMEM((n_e, LANES), F32)],
        compiler_params=_cparams("arbitrary"),
        name="moe_route",
    )(logits_t, router_b.astype(F32).reshape(n_e, 1), tri)


def _routing_plan(top_e, rank, counts_f):
    t = top_e.shape[1]
    counts = counts_f[:, 0].astype(jnp.int32)
    pcounts = (counts + MOE_TM - 1) // MOE_TM * MOE_TM
    pends = jnp.cumsum(pcounts)
    pstarts = pends - pcounts
    dest = (jnp.take(pstarts, top_e) + rank).reshape(-1)
    n_blocks = t * TOP_K // MOE_TM + N_EXPERTS
    n_used = (pends[-1] // MOE_TM).astype(jnp.int32)
    blk_start = jnp.minimum(jnp.arange(n_blocks, dtype=jnp.int32), n_used - 1) * MOE_TM
    blk_e = jnp.sum((pends[None, :] <= blk_start[:, None]).astype(jnp.int32), axis=1)
    blk_e = jnp.minimum(blk_e, N_EXPERTS - 1).astype(jnp.int32)
    return dest, blk_e, n_used.reshape(1), (pstarts + counts).astype(jnp.int32), (pcounts - counts).astype(jnp.int32)


def _dispatch_kernel(dest_ref, pad_start, pad_len, n_used, hp_ref, xs_hbm, zbuf, sem, zsem):
    i = pl.program_id(0)
    tm = hp_ref.shape[0]
    t_total = pl.num_programs(0) * tm
    n_blocks = xs_hbm.shape[0] // MOE_TM

    def pad_copy(e, j):
        return pltpu.make_async_copy(zbuf.at[pl.ds(0, 1), :], xs_hbm.at[pl.ds(pad_start[e] + j, 1), :], zsem)

    def tail_copy(blk):
        row0 = pl.multiple_of(blk * MOE_TM, MOE_TM)
        return pltpu.make_async_copy(zbuf, xs_hbm.at[pl.ds(row0, MOE_TM), :], zsem)

    def for_each_pad_row(fn):
        def per_expert(e, carry):
            def per_row(j, c):
                fn(pad_copy(e, j))
                return c
            return lax.fori_loop(0, pad_len[e], per_row, carry)
        lax.fori_loop(0, N_EXPERTS, per_expert, 0)

        def per_block(blk, carry):
            fn(tail_copy(blk))
            return carry
        lax.fori_loop(n_used[0], n_blocks, per_block, 0)

    @pl.when(i == 0)
    def _():
        zbuf[...] = jnp.zeros(zbuf.shape, zbuf.dtype)
        for_each_pad_row(lambda cp: cp.start())

    def issue(r, carry):
        src = hp_ref.at[pl.ds(r, 1), :]
        for k in range(TOP_K):
            d = dest_ref[k * t_total + i * tm + r]
            pltpu.make_async_copy(src, xs_hbm.at[pl.ds(d, 1), :], sem).start()
        return carry

    lax.fori_loop(0, tm, issue, 0)
    for k in range(TOP_K):
        pltpu.make_async_copy(hp_ref, xs_hbm.at[pl.ds(0, tm), :], sem).wait()

    @pl.when(i == 0)
    def _():
        for_each_pad_row(lambda cp: cp.wait())


def _moe_dispatch(hp, dest, pad_start, pad_len, n_used, n_rows):
    t, dh = hp.shape
    tm = DISP_TM
    grid_spec = pltpu.PrefetchScalarGridSpec(
        num_scalar_prefetch=4,
        grid=(t // tm,),
        in_specs=[pl.BlockSpec((tm, dh), lambda i, ds_, ps, pn, nu: (i, 0))],
        out_specs=pl.BlockSpec(memory_space=pl.ANY),
        scratch_shapes=[pltpu.VMEM((MOE_TM, dh), jnp.uint32), pltpu.SemaphoreType.DMA,
                        pltpu.SemaphoreType.DMA],
    )
    return pl.pallas_call(
        _dispatch_kernel,
        out_shape=jax.ShapeDtypeStruct((n_rows, dh), jnp.uint32),
        grid_spec=grid_spec,
        compiler_params=_cparams("arbitrary"),
        name="moe_dispatch",
    )(dest, pad_start, pad_len, n_used, hp)


def _moe_ffn_kernel(blk_e, n_used, xs_ref, wg_ref, wu_ref, wd_ref, y_ref, wgb, wub, wdb):
    i = pl.program_id(0)

    @pl.when(i < n_used[0])
    def _():
        @pl.when((i == 0) | (blk_e[i] != blk_e[jnp.maximum(i - 1, 0)]))
        def _():
            wgb[...] = wg_ref[...].astype(BF16)
            wub[...] = wu_ref[...].astype(BF16)
            wdb[...] = wd_ref[...].astype(BF16)

        lo, hi = _unpack_bf16_pairs(xs_ref[...])
        x = jnp.concatenate([lo, hi], axis=1).astype(BF16)
        hid = (_silu(_dot(x, wgb[...])) * _dot(x, wub[...])).astype(BF16)
        y_ref[...] = _pack_bf16_pairs(_dot(hid, wdb[...]))

    @pl.when(i >= n_used[0])
    def _():
        y_ref[...] = jnp.zeros(y_ref.shape, y_ref.dtype)


def _moe_ffn(xs, blk_e, n_used, layer, w_gate, w_up, w_down):
    n_rows, dh = xs.shape
    d = 2 * dh
    n_blocks = blk_e.shape[0]
    ff = w_gate.shape[3]
    tm = MOE_TM
    row_blk = lambda i, be, nu: (jnp.minimum(i, nu[0] - 1), 0)
    grid_spec = pltpu.PrefetchScalarGridSpec(
        num_scalar_prefetch=2,
        grid=(n_blocks,),
        in_specs=[
            pl.BlockSpec((tm, dh), row_blk),
            pl.BlockSpec((None, None, d, ff), lambda i, be, nu: (layer, be[i], 0, 0)),
            pl.BlockSpec((None, None, d, ff), lambda i, be, nu: (layer, be[i], 0, 0)),
            pl.BlockSpec((None, None, ff, d), lambda i, be, nu: (layer, be[i], 0, 0)),
        ],
        out_specs=pl.BlockSpec((tm, dh), lambda i, be, nu: (i, 0)),
        scratch_shapes=[pltpu.VMEM((d, ff), BF16), pltpu.VMEM((d, ff), BF16), pltpu.VMEM((ff, d), BF16)],
    )
    return pl.pallas_call(
        _moe_ffn_kernel,
        out_shape=jax.ShapeDtypeStruct((n_rows, dh), jnp.uint32),
        grid_spec=grid_spec,
        compiler_params=_cparams("arbitrary"),
        name="moe_ffn",
    )(blk_e, n_used, xs, w_gate, w_up, w_down)


def _moe_combine_kernel(dest_ref, yp_hbm, gw_ref, hp_ref, x_ref, mod_ref, sg_ref, su_ref, sd_ref,
                        lng_ref, lnb_ref, o_ref, gbuf, sem):
    i = pl.program_id(0)
    tm = hp_ref.shape[0]
    t_total = pl.num_programs(0) * tm

    def issue(r, carry):
        for k in range(TOP_K):
            d = dest_ref[k * t_total + i * tm + r]
            pltpu.make_async_copy(yp_hbm.at[pl.ds(d, 1), :], gbuf.at[k, pl.ds(r, 1), :], sem).start()
        return carry

    lax.fori_loop(0, tm, issue, 0)

    lo, hi = _unpack_bf16_pairs(hp_ref[...])
    h = jnp.concatenate([lo, hi], axis=1).astype(BF16)
    hid = (_silu(_dot(h, sg_ref[...])) * _dot(h, su_ref[...])).astype(BF16)
    out = _dot(hid, sd_ref[...])
    half = out.shape[1] // 2
    out_lo, out_hi = out[:, :half], out[:, half:]
    for k in range(TOP_K):
        pltpu.make_async_copy(yp_hbm.at[pl.ds(0, tm), :], gbuf.at[k], sem).wait()
    gw = gw_ref[...]
    for k in range(TOP_K):
        lo, hi = _unpack_bf16_pairs(gbuf[k])
        wk = gw[:, k:k + 1]
        out_lo = out_lo + wk * lo
        out_hi = out_hi + wk * hi
    out = jnp.concatenate([out_lo, out_hi], axis=1)
    r = DEEPNORM_ALPHA * x_ref[...] + mod_ref[5:6, :] * out
    o_ref[...] = _layer_norm(r, lng_ref[...], lnb_ref[...])


def _moe_combine(yp, dest, gw, hp, x2d, mod, layer, s_gate, s_up, s_down, ln_g, ln_b, rows_per_mod):
    t, d = x2d.shape
    tm = COMB_TM
    sf = s_gate.shape[2]
    tiles_per_mod = rows_per_mod // tm
    grid_spec = pltpu.PrefetchScalarGridSpec(
        num_scalar_prefetch=1,
        grid=(t // tm,),
        in_specs=[
            pl.BlockSpec(memory_space=pl.ANY),
            pl.BlockSpec((tm, TOP_K), lambda i, p: (i, 0)),
            pl.BlockSpec((tm, d // 2), lambda i, p: (i, 0)),
            pl.BlockSpec((tm, d), lambda i, p: (i, 0)),
            pl.BlockSpec((None, SUBLANES, d), lambda i, p: (i // tiles_per_mod, 0, 0)),
            pl.BlockSpec((None, d, sf), lambda i, p: (layer, 0, 0)),
            pl.BlockSpec((None, d, sf), lambda i, p: (layer, 0, 0)),
            pl.BlockSpec((None, sf, d), lambda i, p: (layer, 0, 0)),
            pl.BlockSpec((1, d), lambda i, p: (0, 0)),
            pl.BlockSpec((1, d), lambda i, p: (0, 0)),
        ],
        out_specs=pl.BlockSpec((tm, d), lambda i, p: (i, 0)),
        scratch_shapes=[pltpu.VMEM((TOP_K, tm, d // 2), jnp.uint32), pltpu.SemaphoreType.DMA],
    )
    return pl.pallas_call(
        _moe_combine_kernel,
        out_shape=jax.ShapeDtypeStruct((t, d), F32),
        grid_spec=grid_spec,
        compiler_params=_cparams("arbitrary"),
        name="moe_combine",
    )(dest, yp, gw, hp, x2d, mod, s_gate, s_up, s_down, ln_g.reshape(1, d), ln_b.reshape(1, d))


def _moe_layer(x2d, hp, logits_t, mod, rows_per_mod, layer, router_b, w_gate, w_up, w_down,
               s_gate, s_up, s_down, ln_g, ln_b):
    top_e, gw, rank, counts = _route(logits_t, router_b)
    dest, blk_e, n_used, pad_start, pad_len = _routing_plan(top_e, rank, counts)
    xs = _moe_dispatch(hp, dest, pad_start, pad_len, n_used, blk_e.shape[0] * MOE_TM)
    yp = _moe_ffn(xs, blk_e, n_used, layer, w_gate, w_up, w_down)
    return _moe_combine(yp, dest, gw.T, hp, x2d, mod, layer, s_gate, s_up, s_down, ln_g, ln_b, rows_per_mod)


def _hy_inproj_kernel(x_ref, mod_ref, w_ref, o_ref):
    h = (x_ref[...] * (1.0 + mod_ref[1:2, :]) + mod_ref[0:1, :]).astype(BF16)
    o_ref[...] = _dot(h, w_ref[...])


def _hy_inproj(x2d, mod, w3_bf16, rows_per_mod):
    rows, d = x2d.shape
    n_out = w3_bf16.shape[0]
    tm = HY_TM
    tiles_per_mod = rows_per_mod // tm
    return pl.pallas_call(
        _hy_inproj_kernel,
        out_shape=jax.ShapeDtypeStruct((n_out, rows, d), F32),
        grid=(n_out, rows // tm),
        in_specs=[
            pl.BlockSpec((tm, d), lambda j, i: (i, 0)),
            pl.BlockSpec((None, SUBLANES, d), lambda j, i: (i // tiles_per_mod, 0, 0)),
            pl.BlockSpec((None, d, d), lambda j, i: (j, 0, 0)),
        ],
        out_specs=pl.BlockSpec((None, tm, d), lambda j, i: (j, i, 0)),
        compiler_params=_cparams("arbitrary", "arbitrary"),
        name="hyena_inproj",
    )(x2d, mod, w3_bf16)


def _short_conv_kernel(u_ref, w_ref, b_ref, o_ref):
    u = u_ref[...]
    seq = u.shape[0]
    t = lax.broadcasted_iota(jnp.int32, (seq, 1), 0)
    prev = jnp.where(t == 0, 0.0, pltpu.roll(u, 1, 0))
    nxt = jnp.where(t == seq - 1, 0.0, pltpu.roll(u, seq - 1, 0))
    o_ref[...] = prev * w_ref[0:1, :] + u * w_ref[1:2, :] + nxt * w_ref[2:3, :] + b_ref[...]


def _short_conv(u4, conv_w, conv_b):
    n_out, b, seq, d = u4.shape
    tc = CONV_TC
    return pl.pallas_call(
        _short_conv_kernel,
        out_shape=jax.ShapeDtypeStruct(u4.shape, F32),
        grid=(n_out, b, d // tc),
        in_specs=[
            pl.BlockSpec((None, None, seq, tc), lambda j, i, c: (j, i, 0, c)),
            pl.BlockSpec((None, SUBLANES, tc), lambda j, i, c: (j, 0, c)),
            pl.BlockSpec((None, 1, tc), lambda j, i, c: (j, 0, c)),
        ],
        out_specs=pl.BlockSpec((None, None, seq, tc), lambda j, i, c: (j, i, 0, c)),
        compiler_params=_cparams("arbitrary", "arbitrary", "arbitrary"),
        name="hyena_short_conv",
    )(u4, conv_w, conv_b)


def _filter_kernel(z_ref, t_ref, keep_ref, w1_ref, b1_ref, w2_ref, b2_ref, w3_ref, b3_ref, fq_ref,
                   w4_ref, dl_ref, o_ref):
    fq = fq_ref[...]
    a = jnp.sin(fq * (_dot3(z_ref[...], w1_ref[...]) + b1_ref[...]))
    a = jnp.sin(fq * (_dot3(a, w2_ref[...]) + b2_ref[...]))
    a = jnp.sin(fq * (_dot3(a, w3_ref[...]) + b3_ref[...]))
    hf = _dot3(a, w4_ref[...])
    o_ref[...] = hf * jnp.exp(-t_ref[...] * jnp.abs(dl_ref[...])) * keep_ref[...]


def _hyena_filter(seq, d, f_w1, f_b1, f_w2, f_b2, f_w3, f_b3, f_w4, f_freq):
    n = 2 * seq
    t = jnp.linspace(0.0, 1.0, seq, dtype=F32)[:, None]
    bands = (HY_EMB - 1) // 2
    w = (2.0 * math.pi / seq) * jnp.arange(seq, dtype=F32)
    fb = jnp.linspace(1e-4, bands - 1, bands, dtype=F32)
    fw = w[:, None] * fb[None, :]
    z = jnp.concatenate([t, jnp.cos(fw), -jnp.sin(fw)], axis=-1)
    m = jnp.arange(n)
    src = jnp.where(m < seq, m, jnp.clip(n - m, 0, seq - 1))
    wpad = HY_FILTER_WIDTH - HY_EMB
    z_full = jnp.pad(z[src], ((0, 0), (0, wpad)))
    t_full = t[src]
    keep = (m != seq).astype(F32)[:, None]
    w1p = jnp.pad(f_w1.astype(F32), ((0, wpad), (0, 0)))
    deltas = jnp.linspace(math.log(HY_DECAY_TARGET) / HY_DECAY_FAST,
                          math.log(HY_DECAY_TARGET) / HY_DECAY_SLOW, d, dtype=F32)[None, :]
    fw_ = HY_FILTER_WIDTH
    tiles_per_dir = seq // FILT_TM
    row = lambda v: v.astype(F32).reshape(1, -1)
    small = lambda shape: pl.BlockSpec(shape, lambda i, o: (0, 0))
    return pl.pallas_call(
        _filter_kernel,
        out_shape=jax.ShapeDtypeStruct((HY_ORDER, n, d), F32),
        grid=(n // FILT_TM, HY_ORDER),
        in_specs=[
            pl.BlockSpec((FILT_TM, fw_), lambda i, o: (i, 0)),
            pl.BlockSpec((FILT_TM, 1), lambda i, o: (i, 0)),
            pl.BlockSpec((FILT_TM, 1), lambda i, o: (i, 0)),
            small((fw_, fw_)), small((1, fw_)), small((fw_, fw_)), small((1, fw_)),
            small((fw_, fw_)), small((1, fw_)), small((1, fw_)),
            pl.BlockSpec((fw_, d), lambda i, o: (0, (i // tiles_per_dir) * HY_ORDER + o)),
            small((1, d)),
        ],
        out_specs=pl.BlockSpec((None, FILT_TM, d), lambda i, o: (o, i, 0)),
        compiler_params=_cparams("arbitrary", "arbitrary"),
        name="hyena_filter",
    )(z_full, t_full, keep, w1p, row(f_b1), f_w2.astype(F32), row(f_b2), f_w3.astype(F32), row(f_b3),
      row(f_freq), f_w4.astype(F32), deltas)


def _dft_tables():
    n = FFT_N1 * FFT_N2
    k1 = np.arange(FFT_K1)
    n1 = np.arange(FFT_N1)
    ang1 = 2.0 * np.pi * ((k1[:, None] * n1[None, :]) % FFT_N1) / FFT_N1
    f1 = np.zeros((2 * FFT_K1P, FFT_N1), np.float32)
    f1[:FFT_K1] = np.cos(ang1)
    f1[FFT_K1P:FFT_K1P + FFT_K1] = -np.sin(ang1)
    coef = np.where((k1 == 0) | (k1 == FFT_N1 // 2), 1.0, 2.0) / n
    g = np.zeros((FFT_N1 // 2, 2 * FFT_K1P), np.float32)
    ang1h = ang1[:, :FFT_N1 // 2].T
    g[:, :FFT_K1] = np.cos(ang1h) * coef
    g[:, FFT_K1P:FFT_K1P + FFT_K1] = -np.sin(ang1h) * coef
    n2 = np.arange(FFT_N2)
    k2 = np.arange(FFT_N2)
    mf = np.zeros((FFT_K1P, 2 * FFT_N2, 2 * FFT_N2), np.float32)
    mi = np.zeros((FFT_K1P, 2 * FFT_N2, 2 * FFT_N2), np.float32)
    for a in range(FFT_K1):
        k = a + FFT_N1 * k2
        ang = 2.0 * np.pi * ((k[:, None] * n2[None, :]) % n) / n
        er, ei = np.cos(ang), -np.sin(ang)
        mf[a] = np.block([[er, -ei], [ei, er]])
        mi[a] = np.block([[er.T, ei.T], [-ei.T, er.T]])
    return f1, g, mf, mi


def _dft_stage1_kernel(x_ref, f_ref, re_ref, im_ref):
    f = f_ref[...]
    for j in range(x_ref.shape[1]):
        y = _dot(f, x_ref[:, j, :].astype(BF16))
        re_ref[:, j, :] = y[:FFT_K1P]
        im_ref[:, j, :] = y[FFT_K1P:]


def _dft_stage1(x5, lead, f1):
    _, b, n1, n2, d = x5.shape
    spec_o = pl.BlockSpec((None, FFT_K1P, SUBLANES, d), lambda i, j: (i, 0, j, 0))
    return pl.pallas_call(
        _dft_stage1_kernel,
        out_shape=(jax.ShapeDtypeStruct((b, FFT_K1P, n2, d), F32),) * 2,
        grid=(b, n2 // SUBLANES),
        in_specs=[pl.BlockSpec((None, None, n1, SUBLANES, d), lambda i, j: (lead, i, 0, j, 0)),
                  pl.BlockSpec(f1.shape, lambda i, j: (0, 0))],
        out_specs=(spec_o, spec_o),
        compiler_params=_cparams("arbitrary", "arbitrary"),
        name="dft_stage1",
    )(x5, f1)


def _dft_stage2_kernel(ar_ref, ai_ref, mf_ref, xr_ref, xi_ref):
    a = jnp.concatenate([ar_ref[...], ai_ref[...]], axis=0).astype(BF16)
    x = _dot(mf_ref[...], a)
    xr_ref[...] = x[:FFT_N2]
    xi_ref[...] = x[FFT_N2:]


def _dft_stage2(ar, ai, mf):
    b, _, n2, d = ar.shape
    spec = pl.BlockSpec((None, None, n2, d), lambda k, i: (i, k, 0, 0))
    return pl.pallas_call(
        _dft_stage2_kernel,
        out_shape=(jax.ShapeDtypeStruct(ar.shape, F32),) * 2,
        grid=(FFT_K1P, b),
        in_specs=[spec, spec, pl.BlockSpec((None, 2 * n2, 2 * n2), lambda k, i: (k, 0, 0))],
        out_specs=(spec, spec),
        compiler_params=_cparams("arbitrary", "arbitrary"),
        name="dft_stage2",
    )(ar, ai, mf)


def _spectral_kernel(ar_ref, ai_ref, hr_ref, hi_ref, mf_ref, mi_ref, br_ref, bi_ref):
    a = jnp.concatenate([ar_ref[...], ai_ref[...]], axis=0).astype(BF16)
    x = _dot(mf_ref[...], a)
    xr, xi = x[:FFT_N2], x[FFT_N2:]
    hr, hi = hr_ref[...], hi_ref[...]
    y = jnp.concatenate([xr * hr - xi * hi, xr * hi + xi * hr], axis=0).astype(BF16)
    bv = _dot(mi_ref[...], y)
    br_ref[...] = bv[:FFT_N2]
    bi_ref[...] = bv[FFT_N2:]


def _spectral_multiply(ar, ai, hr, hi, order, mf, mi):
    b, _, n2, d = ar.shape
    spec = pl.BlockSpec((None, None, n2, d), lambda k, i: (i, k, 0, 0))
    spec_h = pl.BlockSpec((None, None, n2, d), lambda k, i: (order, k, 0, 0))
    spec_m = pl.BlockSpec((None, 2 * n2, 2 * n2), lambda k, i: (k, 0, 0))
    return pl.pallas_call(
        _spectral_kernel,
        out_shape=(jax.ShapeDtypeStruct(ar.shape, F32),) * 2,
        grid=(FFT_K1P, b),
        in_specs=[spec, spec, spec_h, spec_h, spec_m, spec_m],
        out_specs=(spec, spec),
        compiler_params=_cparams("arbitrary", "arbitrary"),
        name="dft_spectral",
    )(ar, ai, hr, hi, mf, mi)


def _idft_gate_kernel(br_ref, bi_ref, gr_ref, gi_ref, z_ref, gate_ref, bias_ref, o_ref):
    gr, gi = gr_ref[...], gi_ref[...]
    bias = bias_ref[...]
    for j in range(z_ref.shape[1]):
        y = _dot(gr, br_ref[:, j, :].astype(BF16)) + _dot(gi, bi_ref[:, j, :].astype(BF16))
        o_ref[:, j, :] = gate_ref[:, j, :] * (y + z_ref[:, j, :] * bias)


def _idft_gate(br, bi, gr, gi, z5, z_lead, gate5, gate_lead, bias):
    _, b, n1h, n2, d = z5.shape
    spec_b = pl.BlockSpec((None, FFT_K1P, SUBLANES, d), lambda i, j: (i, 0, j, 0))
    spec_t = pl.BlockSpec(gr.shape, lambda i, j: (0, 0))
    lead_spec = lambda lead: pl.BlockSpec((None, None, n1h, SUBLANES, d), lambda i, j: (lead, i, 0, j, 0))
    return pl.pallas_call(
        _idft_gate_kernel,
        out_shape=jax.ShapeDtypeStruct((1, b, n1h, n2, d), F32),
        grid=(b, n2 // SUBLANES),
        in_specs=[spec_b, spec_b, spec_t, spec_t, lead_spec(z_lead), lead_spec(gate_lead),
                  pl.BlockSpec((1, d), lambda i, j: (0, 0))],
        out_specs=lead_spec(0),
        compiler_params=_cparams("arbitrary", "arbitrary"),
        name="idft_gate",
    )(br, bi, gr, gi, z5, gate5, bias)


def _hyena_mixer(x2d, mod, batch, seq, w_in, conv_w, conv_b, f_w1, f_b1, f_w2, f_b2, f_w3, f_b3, f_w4,
                 f_freq, bias_d):
    d = x2d.shape[1]
    assert 2 * seq == FFT_N1 * FFT_N2
    n1h = FFT_N1 // 2
    w3 = w_in.reshape(d, 3, d).transpose(1, 0, 2).astype(BF16)
    u = _hy_inproj(x2d, mod, w3, seq)
    cw = jnp.pad(conv_w.reshape(3, 3, d).transpose(1, 0, 2), ((0, 0), (0, SUBLANES - 3), (0, 0)))
    cb = conv_b.reshape(3, 1, d)
    u = _short_conv(u.reshape(3, batch, seq, d), cw, cb)
    u5 = u.reshape(3, batch, n1h, FFT_N2, d)
    f1, g, mf, mi = _dft_tables()
    f1 = jnp.asarray(f1).astype(BF16)
    g = jnp.asarray(g).astype(BF16)
    gr, gi = g[:, :FFT_K1P], g[:, FFT_K1P:]
    mf = jnp.asarray(mf).astype(BF16)
    mi = jnp.asarray(mi).astype(BF16)
    kern = _hyena_filter(seq, d, f_w1, f_b1, f_w2, f_b2, f_w3, f_b3, f_w4, f_freq)
    kr, ki = _dft_stage1(kern.reshape(1, HY_ORDER, FFT_N1, FFT_N2, d), 0, f1)
    hr, hi = _dft_stage2(kr, ki, mf)
    z5, z_lead = u5, 0
    f1h = f1[:, :n1h]
    for o in range(HY_ORDER):
        ar, ai = _dft_stage1(z5, z_lead, f1h)
        br, bi = _spectral_multiply(ar, ai, hr, hi, o, mf, mi)
        z5 = _idft_gate(br, bi, gr, gi, z5, z_lead, u5, 1 + o, bias_d[o].astype(F32).reshape(1, d))
        z_lead = 0
    return z5.reshape(batch * seq, d)


def _mod_table(m_layer, d):
    m6 = m_layer.reshape(m_layer.shape[0], 6, d)
    return jnp.pad(m6, ((0, 0), (0, SUBLANES - 6), (0, 0)))


def kernel(x, c, ctx, c_ctx, ada_w, ada_b, ln_g, ln_b, attn_w_in, attn_w_out, attn_sink, attn_q_gain, attn_k_gain, hy_w_in, hy_conv_w, hy_conv_b, hy_f_w1, hy_f_b1, hy_f_w2, hy_f_b2, hy_f_w3, hy_f_b3, hy_f_w4, hy_f_freq, hy_bias_d, hy_w_out, moe_router_w, moe_router_b, moe_w_gate, moe_w_up, moe_w_down, moe_s_gate, moe_s_up, moe_s_down):
    batch, seq, d = x.shape
    n_ctx = ctx.shape[1]
    depth = ada_w.shape[0]
    assert depth == DEPTH and HEAD_DIM == LANES and batch + 1 <= SUBLANES
    cc = jnp.concatenate([c, c_ctx[None, :], jnp.zeros((SUBLANES - batch - 1, d), c.dtype)], axis=0)
    m_all = _ada_modulation(cc.astype(F32), ada_w, ada_b)
    x2d = x.reshape(batch * seq, d)
    cos_t, sin_t = _rope_tables(seq)
    s_gate, s_up, s_down = moe_s_gate.astype(BF16), moe_s_up.astype(BF16), moe_s_down.astype(BF16)
    for l in range(depth):
        mod = _mod_table(m_all[l], d)
        i = l // 2
        router_wt = moe_router_w[l].astype(F32).T
        if l % 2 == 0:
            w_in = attn_w_in[i].astype(BF16)
            qkv = _attn_inproj(x2d, mod, w_in, cos_t, sin_t, attn_q_gain[i], attn_k_gain[i],
                               seq, 0, INPROJ_TM)
            ones = jnp.ones((n_ctx, HEAD_DIM), F32)
            qkv_ctx = _attn_inproj(ctx.reshape(batch * n_ctx, d), mod, w_in, ones, jnp.zeros_like(ones),
                                   attn_q_gain[i], attn_k_gain[i], batch * n_ctx, batch, n_ctx)
            qkv = qkv.reshape(batch, seq, -1)
            qkv_ctx = qkv_ctx.reshape(batch, n_ctx, -1)
            o_a = _win_attention(qkv, qkv_ctx, attn_sink[i].astype(F32))
            o_b = _dense_attention(qkv, qkv_ctx)
            w_out = attn_w_out[i].astype(BF16)
            na = A_Q_HEADS * HEAD_DIM
            acts = [o_a.reshape(batch * seq, na), o_b.reshape(batch * seq, -1)]
            weights = [w_out[:na], w_out[na:]]
        else:
            z = _hyena_mixer(x2d, mod, batch, seq, hy_w_in[i], hy_conv_w[i], hy_conv_b[i], hy_f_w1[i],
                             hy_f_b1[i], hy_f_w2[i], hy_f_b2[i], hy_f_w3[i], hy_f_b3[i], hy_f_w4[i],
                             hy_f_freq[i], hy_bias_d[i])
            acts = [z]
            weights = [hy_w_out[i].astype(BF16)]
        x2d, hp, logits_t = _outproj_ln(acts, weights, x2d, mod, ln_g[l, 0], ln_b[l, 0], router_wt, seq)
        x2d = _moe_layer(x2d, hp, logits_t, mod, seq, l, moe_router_b[l], moe_w_gate, moe_w_up, moe_w_down,
                         s_gate, s_up, s_down, ln_g[l, 1], ln_b[l, 1])
    return x2d.reshape(batch, seq, d)
```

```python
import functools
import math

import numpy as np
import jax
import jax.numpy as jnp
from jax import lax
from jax.experimental import pallas as pl
from jax.experimental.pallas import tpu as pltpu

F32 = jnp.float32
BF16 = jnp.bfloat16

DEPTH = 2
GRID_W = 64
HEAD_DIM = 128
A_Q_HEADS, A_KV_HEADS, B_Q_HEADS, B_KV_HEADS = 8, 2, 8, 2
GQA_GROUP = A_Q_HEADS // A_KV_HEADS
WINDOW = 128
Q_BLOCK = 128
ROPE_THETA = 10000.0
QK_NORM_EPS = 1e-6
LN_EPS = 1e-5
HY_ORDER = 2
HY_EMB = 33
HY_FILTER_WIDTH = 64
HY_DECAY_FAST, HY_DECAY_SLOW, HY_DECAY_TARGET = 0.3, 1.5, 1e-2
N_EXPERTS = 64
TOP_K = 8
N_GROUPS = 8
TOP_GROUPS = 4
ROUTED_SCALE = 2.5
DEEPNORM_ALPHA = (2 * DEPTH) ** 0.25

LANES = 128
SUBLANES = 8
VMEM_LIMIT = 56 * 1024 * 1024

ADA_TN = 1024
INPROJ_TM = 512
DENSE_TQ = 256
DENSE_TK = 1024
OUTPROJ_TM = 256
HY_TM = 512
CONV_TC = 512
MOE_TM = 512
MOE_CHAINS = 1
ROUTE_TT = 512
DISP_TM = 256
COMB_TM = 256
FFT_N1, FFT_N2 = 64, 128
FFT_K1 = FFT_N1 // 2 + 1
FFT_K1P = 40
FFT_NB = 8
FILT_TM = 512


def _cparams(*sem):
    return pltpu.CompilerParams(dimension_semantics=sem, vmem_limit_bytes=VMEM_LIMIT)


def _silu(v):
    return v * jax.nn.sigmoid(v)


def _split_bf16(v):
    hi = v.astype(BF16)
    lo = (v - hi.astype(F32)).astype(BF16)
    return hi, lo


def _dot(a, b):
    return jnp.dot(a, b, preferred_element_type=F32)


def _dot_nt(a, b):
    return lax.dot_general(a, b, (((1,), (1,)), ((), ())), preferred_element_type=F32)


def _pack_bf16_pairs(v):
    n = v.shape[1] // 2
    bits = pltpu.bitcast(v.astype(BF16).astype(F32), jnp.uint32)
    return (bits[:, :n] >> 16) | (bits[:, n:] & jnp.uint32(0xFFFF0000))


def _unpack_bf16_pairs(p):
    lo = pltpu.bitcast(p << 16, F32)
    hi = pltpu.bitcast(p & jnp.uint32(0xFFFF0000), F32)
    return lo, hi


def _dot3(a, b):
    ah, al = _split_bf16(a)
    bh, bl = _split_bf16(b)
    return _dot(ah, bh) + _dot(al, bh) + _dot(ah, bl)


def _ada_kernel(c_ref, w_ref, b_ref, o_ref):
    s = _silu(c_ref[...]).astype(BF16)
    o_ref[...] = _dot(s, w_ref[...].astype(BF16)) + b_ref[...]


def _ada_modulation(cc, ada_w, ada_b):
    depth, d, n = ada_w.shape
    rows = cc.shape[0]
    return pl.pallas_call(
        _ada_kernel,
        out_shape=jax.ShapeDtypeStruct((depth, rows, n), F32),
        grid=(depth, n // ADA_TN),
        in_specs=[
            pl.BlockSpec((rows, d), lambda l, j: (0, 0)),
            pl.BlockSpec((None, d, ADA_TN), lambda l, j: (l, 0, j)),
            pl.BlockSpec((None, 1, ADA_TN), lambda l, j: (l, 0, j)),
        ],
        out_specs=pl.BlockSpec((None, rows, ADA_TN), lambda l, j: (l, 0, j)),
        compiler_params=_cparams("arbitrary", "arbitrary"),
        name="ada_modulation",
    )(cc, ada_w, ada_b.reshape(depth, 1, n))


_N_SLOTS = (A_Q_HEADS + 2 * A_KV_HEADS + B_Q_HEADS + 2 * B_KV_HEADS)
_QA0, _KA0, _VA0 = 0, A_Q_HEADS, A_Q_HEADS + A_KV_HEADS
_QB0 = A_Q_HEADS + 2 * A_KV_HEADS
_KB0, _VB0 = _QB0 + B_Q_HEADS, _QB0 + B_Q_HEADS + B_KV_HEADS


def _slot_kind(j):
    if j < _KA0:
        return "qa"
    if j < _VA0:
        return "ka"
    if j < _QB0:
        return "v"
    if j < _KB0:
        return "qb"
    if j < _VB0:
        return "kb"
    return "v"


def _attn_inproj_kernel(x_ref, mod_ref, w_ref, cos_ref, sin_ref, qg_ref, kg_ref, o_ref):
    x = x_ref[...]
    h = (x * (1.0 + mod_ref[1:2, :]) + mod_ref[0:1, :]).astype(BF16)
    y = _dot(h, w_ref[...])
    cos = cos_ref[...]
    sin = sin_ref[...]
    lane = lax.broadcasted_iota(jnp.int32, cos.shape, 1)
    low_half = (lane & (HEAD_DIM // 2 - 1)) < (HEAD_DIM // 4)
    q_scale = HEAD_DIM ** -0.5
    for j in range(_N_SLOTS):
        kind = _slot_kind(j)
        blk = y[:, j * HEAD_DIM:(j + 1) * HEAD_DIM]
        if kind in ("qb", "kb"):
            gain = qg_ref[...] if kind == "qb" else kg_ref[...]
            ms = jnp.mean(blk * blk, axis=-1, keepdims=True)
            blk = blk * lax.rsqrt(ms + QK_NORM_EPS) * gain
        if kind != "v":
            partner = jnp.where(low_half,
                                pltpu.roll(blk, HEAD_DIM - HEAD_DIM // 4, 1),
                                pltpu.roll(blk, HEAD_DIM // 4, 1))
            blk = blk * cos + partner * sin
        if kind in ("qa", "qb"):
            blk = blk * q_scale
        o_ref[:, j * HEAD_DIM:(j + 1) * HEAD_DIM] = blk.astype(o_ref.dtype)


def _attn_inproj(x2d, mod, w_bf16, cos_t, sin_t, q_gain, k_gain, rows_per_mod, mod_base, tm):
    rows, d = x2d.shape
    n = w_bf16.shape[1]
    tiles_per_mod = rows_per_mod // tm
    tab_tiles = cos_t.shape[0] // tm
    return pl.pallas_call(
        _attn_inproj_kernel,
        out_shape=jax.ShapeDtypeStruct((rows, n), BF16),
        grid=(rows // tm,),
        in_specs=[
            pl.BlockSpec((tm, d), lambda i: (i, 0)),
            pl.BlockSpec((None, SUBLANES, d), lambda i: (mod_base + i // tiles_per_mod, 0, 0)),
            pl.BlockSpec((d, n), lambda i: (0, 0), pipeline_mode=pl.Buffered(1)),
            pl.BlockSpec((tm, HEAD_DIM), lambda i: (i % tab_tiles, 0)),
            pl.BlockSpec((tm, HEAD_DIM), lambda i: (i % tab_tiles, 0)),
            pl.BlockSpec((1, HEAD_DIM), lambda i: (0, 0)),
            pl.BlockSpec((1, HEAD_DIM), lambda i: (0, 0)),
        ],
        out_specs=pl.BlockSpec((tm, n), lambda i: (i, 0)),
        compiler_params=_cparams("arbitrary"),
        name="attn_inproj",
    )(x2d, mod, w_bf16, cos_t, sin_t, q_gain.reshape(1, -1), k_gain.reshape(1, -1))


def _rope_tables(seq):
    rows = seq // GRID_W
    row = jnp.repeat(jnp.arange(rows, dtype=F32), GRID_W)
    col = jnp.tile(jnp.arange(GRID_W, dtype=F32), rows)
    n_freq = HEAD_DIM // 4
    inv = ROPE_THETA ** (-jnp.arange(n_freq, dtype=F32) / n_freq)
    ang_r, ang_c = row[:, None] * inv, col[:, None] * inv
    cos = jnp.concatenate([jnp.cos(ang_r)] * 2 + [jnp.cos(ang_c)] * 2, axis=-1)
    sin = jnp.concatenate([-jnp.sin(ang_r), jnp.sin(ang_r), -jnp.sin(ang_c), jnp.sin(ang_c)], axis=-1)
    return cos, sin


def _win_attn_kernel(sink_ref, q_ref, k_ref, v_ref, kc_ref, vc_ref, o_ref):
    seq = q_ref.shape[0]
    hk = pl.program_id(1)
    n_loc = 3 * Q_BLOCK
    rows = GQA_GROUP * Q_BLOCK
    row = lax.broadcasted_iota(jnp.int32, (rows, 1), 0)
    col = lax.broadcasted_iota(jnp.int32, (1, n_loc), 1)
    sink_col = jnp.full((rows, 1), sink_ref[hk * GQA_GROUP + GQA_GROUP - 1], F32)
    for g in range(GQA_GROUP - 2, -1, -1):
        sink_col = jnp.where(row < (g + 1) * Q_BLOCK, sink_ref[hk * GQA_GROUP + g], sink_col)
    kc = kc_ref[...]
    vc = vc_ref[...]

    def body(n, carry):
        q0 = pl.multiple_of(n * Q_BLOCK, Q_BLOCK)
        start = pl.multiple_of(jnp.clip(q0 - Q_BLOCK, 0, seq - n_loc), Q_BLOCK)
        q = jnp.concatenate(
            [q_ref[pl.ds(q0, Q_BLOCK), g * HEAD_DIM:(g + 1) * HEAD_DIM] for g in range(GQA_GROUP)], axis=0)
        kw = k_ref[pl.ds(start, n_loc), :]
        vw = v_ref[pl.ds(start, n_loc), :]
        s_loc = _dot_nt(q, kw)
        s_ctx = _dot_nt(q, kc)
        qpos = q0 + (row & (Q_BLOCK - 1))
        kpos = start + col
        s_loc = jnp.where(jnp.abs(qpos - kpos) <= WINDOW, s_loc, -jnp.inf)
        m = jnp.maximum(jnp.maximum(jnp.max(s_loc, -1, keepdims=True), jnp.max(s_ctx, -1, keepdims=True)),
                        sink_col)
        p_loc = jnp.exp(s_loc - m)
        p_ctx = jnp.exp(s_ctx - m)
        denom = (jnp.sum(p_loc, -1, keepdims=True) + jnp.sum(p_ctx, -1, keepdims=True)
                 + jnp.exp(sink_col - m))
        o = (_dot(p_loc.astype(BF16), vw) + _dot(p_ctx.astype(BF16), vc)) / denom
        for g in range(GQA_GROUP):
            o_ref[pl.ds(q0, Q_BLOCK), g * HEAD_DIM:(g + 1) * HEAD_DIM] = (
                o[g * Q_BLOCK:(g + 1) * Q_BLOCK].astype(o_ref.dtype))
        return carry

    lax.fori_loop(0, seq // Q_BLOCK, body, 0)


def _win_attention(qkv, qkv_ctx, sink):
    b, s, _ = qkv.shape
    c = qkv_ctx.shape[1]
    gw = GQA_GROUP * HEAD_DIM
    return pl.pallas_call(
        _win_attn_kernel,
        out_shape=jax.ShapeDtypeStruct((b, s, A_Q_HEADS * HEAD_DIM), BF16),
        grid=(b, A_KV_HEADS),
        in_specs=[
            pl.BlockSpec(memory_space=pltpu.SMEM),
            pl.BlockSpec((None, s, gw), lambda i, h: (i, 0, h)),
            pl.BlockSpec((None, s, HEAD_DIM), lambda i, h: (i, 0, _KA0 + h)),
            pl.BlockSpec((None, s, HEAD_DIM), lambda i, h: (i, 0, _VA0 + h)),
            pl.BlockSpec((None, c, HEAD_DIM), lambda i, h: (i, 0, _KA0 + h)),
            pl.BlockSpec((None, c, HEAD_DIM), lambda i, h: (i, 0, _VA0 + h)),
        ],
        out_specs=pl.BlockSpec((None, s, gw), lambda i, h: (i, 0, h)),
        compiler_params=_cparams("arbitrary", "arbitrary"),
        name="win_attention",
    )(sink, qkv, qkv, qkv, qkv_ctx, qkv_ctx)


def _dense_attn_kernel(q_ref, k_ref, v_ref, kc_ref, vc_ref, o_ref, m_sc, l_sc, acc_sc):
    seq = k_ref.shape[0]
    m_sc[...] = jnp.full(m_sc.shape, -jnp.inf, F32)
    l_sc[...] = jnp.zeros(l_sc.shape, F32)
    acc_sc[...] = jnp.zeros(acc_sc.shape, F32)

    def step(kb, vb):
        reps = kb.shape[0] // LANES
        for g in range(GQA_GROUP):
            s = _dot_nt(q_ref[:, g * HEAD_DIM:(g + 1) * HEAD_DIM], kb)
            m_prev = m_sc[g]
            m_new = jnp.maximum(m_prev, jnp.max(s, -1, keepdims=True))
            alpha = jnp.exp(m_prev - m_new)
            p = jnp.exp(s - jnp.concatenate([m_new] * reps, axis=1))
            l_sc[g] = alpha * l_sc[g] + jnp.sum(p, -1, keepdims=True)
            acc_sc[g] = alpha * acc_sc[g] + _dot(p.astype(BF16), vb)
            m_sc[g] = m_new

    def body(j, carry):
        k0 = pl.multiple_of(j * DENSE_TK, DENSE_TK)
        step(k_ref[pl.ds(k0, DENSE_TK), :], v_ref[pl.ds(k0, DENSE_TK), :])
        return carry

    lax.fori_loop(0, seq // DENSE_TK, body, 0)
    step(kc_ref[...], vc_ref[...])
    for g in range(GQA_GROUP):
        o_ref[:, g * HEAD_DIM:(g + 1) * HEAD_DIM] = (acc_sc[g] / l_sc[g]).astype(o_ref.dtype)


def _dense_attention(qkv, qkv_ctx):
    b, s, _ = qkv.shape
    c = qkv_ctx.shape[1]
    gw = GQA_GROUP * HEAD_DIM
    qb_blk = _QB0 * HEAD_DIM // gw
    stat = pltpu.VMEM((GQA_GROUP, DENSE_TQ, LANES), F32)
    return pl.pallas_call(
        _dense_attn_kernel,
        out_shape=jax.ShapeDtypeStruct((b, s, B_Q_HEADS * HEAD_DIM), BF16),
        grid=(b, B_KV_HEADS, s // DENSE_TQ),
        in_specs=[
            pl.BlockSpec((None, DENSE_TQ, gw), lambda i, h, t: (i, t, qb_blk + h)),
            pl.BlockSpec((None, s, HEAD_DIM), lambda i, h, t: (i, 0, _KB0 + h)),
            pl.BlockSpec((None, s, HEAD_DIM), lambda i, h, t: (i, 0, _VB0 + h)),
            pl.BlockSpec((None, c, HEAD_DIM), lambda i, h, t: (i, 0, _KB0 + h)),
            pl.BlockSpec((None, c, HEAD_DIM), lambda i, h, t: (i, 0, _VB0 + h)),
        ],
        out_specs=pl.BlockSpec((None, DENSE_TQ, gw), lambda i, h, t: (i, t, h)),
        scratch_shapes=[stat, stat, stat],
        compiler_params=_cparams("arbitrary", "arbitrary", "arbitrary"),
        name="dense_attention",
    )(qkv, qkv, qkv, qkv_ctx, qkv_ctx)


def _layer_norm(r, g, b):
    mu = jnp.mean(r, axis=-1, keepdims=True)
    c = r - mu
    var = jnp.mean(c * c, axis=-1, keepdims=True)
    return c * lax.rsqrt(var + LN_EPS) * g + b


def _outproj_ln_kernel(n_in, *refs):
    a_refs = refs[:n_in]
    w_refs = refs[n_in:2 * n_in]
    x_ref, mod_ref, lng_ref, lnb_ref, rwh_ref, rwl_ref, xo_ref, h2_ref, lg_ref = refs[2 * n_in:]
    y = _dot(a_refs[0][...].astype(BF16), w_refs[0][...])
    for a_ref, w_ref in zip(a_refs[1:], w_refs[1:]):
        y = y + _dot(a_ref[...].astype(BF16), w_ref[...])
    r = DEEPNORM_ALPHA * x_ref[...] + mod_ref[2:3, :] * y
    xn = _layer_norm(r, lng_ref[...], lnb_ref[...])
    xo_ref[...] = xn
    h2 = xn * (1.0 + mod_ref[4:5, :]) + mod_ref[3:4, :]
    h2_ref[...] = _pack_bf16_pairs(h2)
    hi, lo = _split_bf16(h2)
    rwh = rwh_ref[...]
    lg_ref[...] = _dot_nt(rwh, hi) + _dot_nt(rwh, lo) + _dot_nt(rwl_ref[...], hi)


def _outproj_ln(acts, weights, x2d, mod, ln_g, ln_b, router_wt, rows_per_mod):
    rows, d = x2d.shape
    tm = OUTPROJ_TM
    n_in = len(acts)
    n_e = router_wt.shape[0]
    tiles_per_mod = rows_per_mod // tm
    rwh, rwl = _split_bf16(router_wt)
    in_specs = (
        [pl.BlockSpec((tm, a.shape[1]), lambda i: (i, 0)) for a in acts]
        + [pl.BlockSpec(w.shape, lambda i: (0, 0), pipeline_mode=pl.Buffered(1)) for w in weights]
        + [
            pl.BlockSpec((tm, d), lambda i: (i, 0)),
            pl.BlockSpec((None, SUBLANES, d), lambda i: (i // tiles_per_mod, 0, 0)),
            pl.BlockSpec((1, d), lambda i: (0, 0)),
            pl.BlockSpec((1, d), lambda i: (0, 0)),
            pl.BlockSpec((n_e, d), lambda i: (0, 0)),
            pl.BlockSpec((n_e, d), lambda i: (0, 0)),
        ])
    return pl.pallas_call(
        functools.partial(_outproj_ln_kernel, n_in),
        out_shape=(jax.ShapeDtypeStruct((rows, d), F32), jax.ShapeDtypeStruct((rows, d // 2), jnp.uint32),
                   jax.ShapeDtypeStruct((n_e, rows), F32)),
        grid=(rows // tm,),
        in_specs=in_specs,
        out_specs=(pl.BlockSpec((tm, d), lambda i: (i, 0)), pl.BlockSpec((tm, d // 2), lambda i: (i, 0)),
                   pl.BlockSpec((n_e, tm), lambda i: (0, i))),
        compiler_params=_cparams("arbitrary"),
        name="outproj_ln",
    )(*acts, *weights, x2d, mod, ln_g.reshape(1, d), ln_b.reshape(1, d), rwh, rwl)


def _route_kernel(lg_ref, b_ref, tri_ref, e_ref, w_ref, r_ref, cnt_ref, carry_sc):
    @pl.when(pl.program_id(0) == 0)
    def _():
        carry_sc[...] = jnp.zeros(carry_sc.shape, F32)

    tt = lg_ref.shape[1]
    gsz = N_EXPERTS // N_GROUPS
    s = jax.nn.sigmoid(lg_ref[...])
    biased = s + b_ref[...]
    sub = lax.broadcasted_iota(jnp.int32, (gsz, tt), 0).astype(F32)
    rows = []
    for g in range(N_GROUPS):
        v = biased[g * gsz:(g + 1) * gsz]
        m1 = jnp.max(v, axis=0, keepdims=True)
        first = jnp.min(jnp.where(v == m1, sub, float(gsz)), axis=0, keepdims=True)
        m2 = jnp.max(jnp.where(sub == first, -jnp.inf, v), axis=0, keepdims=True)
        rows.append(m1 + m2)
    gs = jnp.concatenate(rows, axis=0)
    gidx = lax.broadcasted_iota(jnp.int32, (N_GROUPS, tt), 0)
    grank = jnp.zeros((N_GROUPS, tt), F32)
    for g2 in range(N_GROUPS):
        row = gs[g2:g2 + 1]
        ahead = jnp.where(row > gs, 1.0, jnp.where(row == gs, jnp.where(gidx > g2, 1.0, 0.0), 0.0))
        grank = grank + ahead
    gkeep = jnp.where(grank < TOP_GROUPS, 1.0, 0.0)
    keep = jnp.concatenate([jnp.broadcast_to(gkeep[g:g + 1], (gsz, tt)) for g in range(N_GROUPS)], axis=0)
    cur = jnp.where(keep > 0.5, biased, -jnp.inf)
    eidx = lax.broadcasted_iota(jnp.int32, (N_EXPERTS, tt), 0).astype(F32)
    chosen = jnp.zeros((N_EXPERTS, tt), F32)
    e_rows, s_rows = [], []
    for _ in range(TOP_K):
        m = jnp.max(cur, axis=0, keepdims=True)
        idx = jnp.min(jnp.where(cur == m, eidx, float(N_EXPERTS)), axis=0, keepdims=True)
        hit = eidx == idx
        s_rows.append(jnp.sum(jnp.where(hit, s, 0.0), axis=0, keepdims=True))
        e_rows.append(idx)
        chosen = jnp.where(hit, 1.0, chosen)
        cur = jnp.where(hit, -jnp.inf, cur)
    top_e = jnp.concatenate(e_rows, axis=0)
    sc = jnp.concatenate(s_rows, axis=0)
    w_ref[...] = sc / jnp.sum(sc, axis=0, keepdims=True) * ROUTED_SCALE
    e_ref[...] = top_e.astype(jnp.int32)
    carry = carry_sc[...]
    cnt = _dot(chosen.astype(BF16), tri_ref[...]) + jnp.concatenate([carry] * (tt // LANES), axis=1)
    rank = jnp.zeros((TOP_K, tt), F32)
    for e in range(N_EXPERTS):
        rank = rank + jnp.where(top_e == float(e), cnt[e:e + 1], 0.0)
    r_ref[...] = rank.astype(jnp.int32)
    carry = carry + jnp.sum(chosen, axis=1, keepdims=True)
    carry_sc[...] = carry
    cnt_ref[...] = carry


def _route(logits_t, router_b):
    n_e, t = logits_t.shape
    tt = ROUTE_TT
    tri = (jnp.arange(tt)[:, None] < jnp.arange(tt)[None, :]).astype(BF16)
    tok_spec = pl.BlockSpec((TOP_K, tt), lambda i: (0, i))
    return pl.pallas_call(
        _route_kernel,
        out_shape=(jax.ShapeDtypeStruct((TOP_K, t), jnp.int32), jax.ShapeDtypeStruct((TOP_K, t), F32),
                   jax.ShapeDtypeStruct((TOP_K, t), jnp.int32), jax.ShapeDtypeStruct((n_e, LANES), F32)),
        grid=(t // tt,),
        in_specs=[pl.BlockSpec((n_e, tt), lambda i: (0, i)),
                  pl.BlockSpec((n_e, 1), lambda i: (0, 0)),
                  pl.BlockSpec((tt, tt), lambda i: (0, 0))],
        out_specs=(tok_spec, tok_spec, tok_spec, pl.BlockSpec((n_e, LANES), lambda i: (0, 0))),
        scratch_shapes=[pltpu.VMEM((n_e, LANES), F32)],
        compiler_params=_cparams("arbitrary"),
        name="moe_route",
    )(logits_t, router_b.astype(F32).reshape(n_e, 1), tri)


def _routing_plan(top_e, rank, counts_f):
    t = top_e.shape[1]
    counts = counts_f[:, 0].astype(jnp.int32)
    pcounts = (counts + MOE_TM - 1) // MOE_TM * MOE_TM
    pends = jnp.cumsum(pcounts)
    pstarts = pends - pcounts
    first_row = jnp.sum(jnp.where(top_e[..., None] == jnp.arange(N_EXPERTS, dtype=jnp.int32), pstarts, 0), axis=-1)
    dest = (first_row + rank).reshape(-1)
    n_blocks = t * TOP_K // MOE_TM + N_EXPERTS
    n_used = (pends[-1] // MOE_TM).astype(jnp.int32)
    blk_start = jnp.minimum(jnp.arange(n_blocks, dtype=jnp.int32), n_used - 1) * MOE_TM
    blk_e = jnp.sum((pends[None, :] <= blk_start[:, None]).astype(jnp.int32), axis=1)
    blk_e = jnp.minimum(blk_e, N_EXPERTS - 1).astype(jnp.int32)
    return dest, blk_e, n_used.reshape(1), (pstarts + counts).astype(jnp.int32), (pcounts - counts).astype(jnp.int32)


_PAD_PIECES = tuple(1 << b for b in range(MOE_TM.bit_length() - 2, SUBLANES.bit_length() - 2, -1))


def _dispatch_kernel(dest_ref, pad_start, pad_len, n_used, hp_ref, xs_hbm, zbuf, sem, zsem):
    i = pl.program_id(0)
    tm = hp_ref.shape[0]
    t_total = pl.num_programs(0) * tm
    n_blocks = xs_hbm.shape[0] // MOE_TM

    def zero_copy(row0, n):
        return pltpu.make_async_copy(zbuf.at[pl.ds(0, n), :], xs_hbm.at[pl.ds(row0, n), :], zsem)

    def for_each_pad_row(fn):
        def per_expert(e, carry):
            start, length = pad_start[e], pad_len[e]
            head = jnp.minimum((-start) & (SUBLANES - 1), length)

            def per_row(j, c):
                fn(zero_copy(start + j, 1))
                return c
            lax.fori_loop(0, head, per_row, 0)
            rest = length - head
            for piece in _PAD_PIECES:
                @pl.when((rest & piece) != 0)
                def _():
                    fn(zero_copy(pl.multiple_of(start + head + (rest & ~(2 * piece - 1)), SUBLANES), piece))
            return carry
        lax.fori_loop(0, N_EXPERTS, per_expert, 0)

        def per_block(blk, carry):
            fn(zero_copy(pl.multiple_of(blk * MOE_TM, MOE_TM), MOE_TM))
            return carry
        lax.fori_loop(n_used[0], n_blocks, per_block, 0)

    @pl.when(i == 0)
    def _():
        zbuf[...] = jnp.zeros(zbuf.shape, zbuf.dtype)
        for_each_pad_row(lambda cp: cp.start())

    def issue(r, carry):
        src = hp_ref.at[pl.ds(r, 1), :]
        for k in range(TOP_K):
            d = dest_ref[k * t_total + i * tm + r]
            pltpu.make_async_copy(src, xs_hbm.at[pl.ds(d, 1), :], sem).start(priority=k % 2)
        return carry

    lax.fori_loop(0, tm, issue, 0)
    for k in range(TOP_K):
        pltpu.make_async_copy(hp_ref, xs_hbm.at[pl.ds(0, tm), :], sem).wait()

    @pl.when(i == 0)
    def _():
        for_each_pad_row(lambda cp: cp.wait())


def _moe_dispatch(hp, dest, pad_start, pad_len, n_used, n_rows):
    t, dh = hp.shape
    tm = DISP_TM
    grid_spec = pltpu.PrefetchScalarGridSpec(
        num_scalar_prefetch=4,
        grid=(t // tm,),
        in_specs=[pl.BlockSpec((tm, dh), lambda i, ds_, ps, pn, nu: (i, 0))],
        out_specs=pl.BlockSpec(memory_space=pl.ANY),
        scratch_shapes=[pltpu.VMEM((MOE_TM, dh), jnp.uint32), pltpu.SemaphoreType.DMA,
                        pltpu.SemaphoreType.DMA],
    )
    return pl.pallas_call(
        _dispatch_kernel,
        out_shape=jax.ShapeDtypeStruct((n_rows, dh), jnp.uint32),
        grid_spec=grid_spec,
        compiler_params=_cparams("arbitrary"),
        name="moe_dispatch",
    )(dest, pad_start, pad_len, n_used, hp)


def _moe_ffn_kernel(blk_e, n_used, xs_ref, wg_ref, wu_ref, wd_ref, y_ref, wgb, wub, wdb):
    i = pl.program_id(0)

    @pl.when(i < n_used[0])
    def _():
        @pl.when((i == 0) | (blk_e[i] != blk_e[jnp.maximum(i - 1, 0)]))
        def _():
            wgb[...] = wg_ref[...].astype(BF16)
            wub[...] = wu_ref[...].astype(BF16)
            wdb[...] = wd_ref[...].astype(BF16)

        rows = xs_ref.shape[0] // MOE_CHAINS
        for c in range(MOE_CHAINS):
            sl = pl.ds(c * rows, rows)
            lo, hi = _unpack_bf16_pairs(xs_ref[sl, :])
            x = jnp.concatenate([lo, hi], axis=1).astype(BF16)
            hid = (_silu(_dot(x, wgb[...])) * _dot(x, wub[...])).astype(BF16)
            y_ref[sl, :] = _pack_bf16_pairs(_dot(hid, wdb[...]))

    @pl.when(i >= n_used[0])
    def _():
        y_ref[...] = jnp.zeros(y_ref.shape, y_ref.dtype)


def _moe_ffn(xs, blk_e, n_used, layer, w_gate, w_up, w_down):
    n_rows, dh = xs.shape
    d = 2 * dh
    n_blocks = blk_e.shape[0]
    ff = w_gate.shape[3]
    tm = MOE_TM
    row_blk = lambda i, be, nu: (jnp.minimum(i, nu[0] - 1), 0)
    grid_spec = pltpu.PrefetchScalarGridSpec(
        num_scalar_prefetch=2,
        grid=(n_blocks,),
        in_specs=[
            pl.BlockSpec((tm, dh), row_blk),
            pl.BlockSpec((None, None, d, ff), lambda i, be, nu: (layer, be[i], 0, 0)),
            pl.BlockSpec((None, None, d, ff), lambda i, be, nu: (layer, be[i], 0, 0)),
            pl.BlockSpec((None, None, ff, d), lambda i, be, nu: (layer, be[i], 0, 0)),
        ],
        out_specs=pl.BlockSpec((tm, dh), lambda i, be, nu: (i, 0)),
        scratch_shapes=[pltpu.VMEM((d, ff), BF16), pltpu.VMEM((d, ff), BF16), pltpu.VMEM((ff, d), BF16)],
    )
    return pl.pallas_call(
        _moe_ffn_kernel,
        out_shape=jax.ShapeDtypeStruct((n_rows, dh), jnp.uint32),
        grid_spec=grid_spec,
        compiler_params=_cparams("arbitrary"),
        name="moe_ffn",
    )(blk_e, n_used, xs, w_gate, w_up, w_down)


def _moe_combine_kernel(dest_ref, yp_hbm, gw_ref, hp_ref, x_ref, mod_ref, sg_ref, su_ref, sd_ref,
                        lng_ref, lnb_ref, o_ref, gbuf, sem):
    i = pl.program_id(0)
    tm = hp_ref.shape[0]
    t_total = pl.num_programs(0) * tm

    def issue(r, carry):
        for k in range(TOP_K):
            d = dest_ref[k * t_total + i * tm + r]
            pltpu.make_async_copy(yp_hbm.at[pl.ds(d, 1), :], gbuf.at[k, pl.ds(r, 1), :], sem).start()
        return carry

    lax.fori_loop(0, tm, issue, 0)

    lo, hi = _unpack_bf16_pairs(hp_ref[...])
    h = jnp.concatenate([lo, hi], axis=1).astype(BF16)
    hid = (_silu(_dot(h, sg_ref[...])) * _dot(h, su_ref[...])).astype(BF16)
    out = _dot(hid, sd_ref[...])
    half = out.shape[1] // 2
    out_lo, out_hi = out[:, :half], out[:, half:]
    for k in range(TOP_K):
        pltpu.make_async_copy(yp_hbm.at[pl.ds(0, tm), :], gbuf.at[k], sem).wait()
    gw = gw_ref[...]
    for k in range(TOP_K):
        lo, hi = _unpack_bf16_pairs(gbuf[k])
        wk = gw[:, k:k + 1]
        out_lo = out_lo + wk * lo
        out_hi = out_hi + wk * hi
    out = jnp.concatenate([out_lo, out_hi], axis=1)
    r = DEEPNORM_ALPHA * x_ref[...] + mod_ref[5:6, :] * out
    o_ref[...] = _layer_norm(r, lng_ref[...], lnb_ref[...])


def _moe_combine(yp, dest, gw, hp, x2d, mod, layer, s_gate, s_up, s_down, ln_g, ln_b, rows_per_mod):
    t, d = x2d.shape
    tm = COMB_TM
    sf = s_gate.shape[2]
    tiles_per_mod = rows_per_mod // tm
    grid_spec = pltpu.PrefetchScalarGridSpec(
        num_scalar_prefetch=1,
        grid=(t // tm,),
        in_specs=[
            pl.BlockSpec(memory_space=pl.ANY),
            pl.BlockSpec((tm, TOP_K), lambda i, p: (i, 0)),
            pl.BlockSpec((tm, d // 2), lambda i, p: (i, 0)),
            pl.BlockSpec((tm, d), lambda i, p: (i, 0)),
            pl.BlockSpec((None, SUBLANES, d), lambda i, p: (i // tiles_per_mod, 0, 0)),
            pl.BlockSpec((None, d, sf), lambda i, p: (layer, 0, 0)),
            pl.BlockSpec((None, d, sf), lambda i, p: (layer, 0, 0)),
            pl.BlockSpec((None, sf, d), lambda i, p: (layer, 0, 0)),
            pl.BlockSpec((1, d), lambda i, p: (0, 0)),
            pl.BlockSpec((1, d), lambda i, p: (0, 0)),
        ],
        out_specs=pl.BlockSpec((tm, d), lambda i, p: (i, 0)),
        scratch_shapes=[pltpu.VMEM((TOP_K, tm, d // 2), jnp.uint32), pltpu.SemaphoreType.DMA],
    )
    return pl.pallas_call(
        _moe_combine_kernel,
        out_shape=jax.ShapeDtypeStruct((t, d), F32),
        grid_spec=grid_spec,
        compiler_params=_cparams("arbitrary"),
        name="moe_combine",
    )(dest, yp, gw, hp, x2d, mod, s_gate, s_up, s_down, ln_g.reshape(1, d), ln_b.reshape(1, d))


def _moe_layer(x2d, hp, logits_t, mod, rows_per_mod, layer, router_b, w_gate, w_up, w_down,
               s_gate, s_up, s_down, ln_g, ln_b):
    top_e, gw, rank, counts = _route(logits_t, router_b)
    dest, blk_e, n_used, pad_start, pad_len = _routing_plan(top_e, rank, counts)
    xs = _moe_dispatch(hp, dest, pad_start, pad_len, n_used, blk_e.shape[0] * MOE_TM)
    yp = _moe_ffn(xs, blk_e, n_used, layer, w_gate, w_up, w_down)
    return _moe_combine(yp, dest, gw.T, hp, x2d, mod, layer, s_gate, s_up, s_down, ln_g, ln_b, rows_per_mod)


def _hy_inproj_kernel(x_ref, mod_ref, w_ref, o_ref):
    h = (x_ref[...] * (1.0 + mod_ref[1:2, :]) + mod_ref[0:1, :]).astype(BF16)
    o_ref[...] = _dot(h, w_ref[...])


def _hy_inproj(x2d, mod, w3_bf16, rows_per_mod):
    rows, d = x2d.shape
    n_out = w3_bf16.shape[0]
    tm = HY_TM
    tiles_per_mod = rows_per_mod // tm
    return pl.pallas_call(
        _hy_inproj_kernel,
        out_shape=jax.ShapeDtypeStruct((n_out, rows, d), F32),
        grid=(n_out, rows // tm),
        in_specs=[
            pl.BlockSpec((tm, d), lambda j, i: (i, 0)),
            pl.BlockSpec((None, SUBLANES, d), lambda j, i: (i // tiles_per_mod, 0, 0)),
            pl.BlockSpec((None, d, d), lambda j, i: (j, 0, 0)),
        ],
        out_specs=pl.BlockSpec((None, tm, d), lambda j, i: (j, i, 0)),
        compiler_params=_cparams("arbitrary", "arbitrary"),
        name="hyena_inproj",
    )(x2d, mod, w3_bf16)


def _short_conv_kernel(u_ref, w_ref, b_ref, o_ref):
    u = u_ref[...]
    seq = u.shape[0]
    t = lax.broadcasted_iota(jnp.int32, (seq, 1), 0)
    prev = jnp.where(t == 0, 0.0, pltpu.roll(u, 1, 0))
    nxt = jnp.where(t == seq - 1, 0.0, pltpu.roll(u, seq - 1, 0))
    o_ref[...] = prev * w_ref[0:1, :] + u * w_ref[1:2, :] + nxt * w_ref[2:3, :] + b_ref[...]


def _short_conv(u4, conv_w, conv_b):
    n_out, b, seq, d = u4.shape
    tc = CONV_TC
    return pl.pallas_call(
        _short_conv_kernel,
        out_shape=jax.ShapeDtypeStruct(u4.shape, F32),
        grid=(n_out, b, d // tc),
        in_specs=[
            pl.BlockSpec((None, None, seq, tc), lambda j, i, c: (j, i, 0, c)),
            pl.BlockSpec((None, SUBLANES, tc), lambda j, i, c: (j, 0, c)),
            pl.BlockSpec((None, 1, tc), lambda j, i, c: (j, 0, c)),
        ],
        out_specs=pl.BlockSpec((None, None, seq, tc), lambda j, i, c: (j, i, 0, c)),
        compiler_params=_cparams("arbitrary", "arbitrary", "arbitrary"),
        name="hyena_short_conv",
    )(u4, conv_w, conv_b)


def _filter_kernel(z_ref, t_ref, keep_ref, w1_ref, b1_ref, w2_ref, b2_ref, w3_ref, b3_ref, fq_ref,
                   w4_ref, dl_ref, o_ref):
    fq = fq_ref[...]
    a = jnp.sin(fq * (_dot3(z_ref[...], w1_ref[...]) + b1_ref[...]))
    a = jnp.sin(fq * (_dot3(a, w2_ref[...]) + b2_ref[...]))
    a = jnp.sin(fq * (_dot3(a, w3_ref[...]) + b3_ref[...]))
    hf = _dot3(a, w4_ref[...])
    o_ref[...] = hf * jnp.exp(-t_ref[...] * jnp.abs(dl_ref[...])) * keep_ref[...]


def _hyena_filter(seq, d, f_w1, f_b1, f_w2, f_b2, f_w3, f_b3, f_w4, f_freq):
    n = 2 * seq
    t = jnp.linspace(0.0, 1.0, seq, dtype=F32)[:, None]
    bands = (HY_EMB - 1) // 2
    w = (2.0 * math.pi / seq) * jnp.arange(seq, dtype=F32)
    fb = jnp.linspace(1e-4, bands - 1, bands, dtype=F32)
    fw = w[:, None] * fb[None, :]
    z = jnp.concatenate([t, jnp.cos(fw), -jnp.sin(fw)], axis=-1)
    m = jnp.arange(n)
    src = jnp.where(m < seq, m, jnp.clip(n - m, 0, seq - 1))
    wpad = HY_FILTER_WIDTH - HY_EMB
    z_full = jnp.pad(z[src], ((0, 0), (0, wpad)))
    t_full = t[src]
    keep = (m != seq).astype(F32)[:, None]
    w1p = jnp.pad(f_w1.astype(F32), ((0, wpad), (0, 0)))
    deltas = jnp.linspace(math.log(HY_DECAY_TARGET) / HY_DECAY_FAST,
                          math.log(HY_DECAY_TARGET) / HY_DECAY_SLOW, d, dtype=F32)[None, :]
    fw_ = HY_FILTER_WIDTH
    tiles_per_dir = seq // FILT_TM
    row = lambda v: v.astype(F32).reshape(1, -1)
    small = lambda shape: pl.BlockSpec(shape, lambda i, o: (0, 0))
    return pl.pallas_call(
        _filter_kernel,
        out_shape=jax.ShapeDtypeStruct((HY_ORDER, n, d), F32),
        grid=(n // FILT_TM, HY_ORDER),
        in_specs=[
            pl.BlockSpec((FILT_TM, fw_), lambda i, o: (i, 0)),
            pl.BlockSpec((FILT_TM, 1), lambda i, o: (i, 0)),
            pl.BlockSpec((FILT_TM, 1), lambda i, o: (i, 0)),
            small((fw_, fw_)), small((1, fw_)), small((fw_, fw_)), small((1, fw_)),
            small((fw_, fw_)), small((1, fw_)), small((1, fw_)),
            pl.BlockSpec((fw_, d), lambda i, o: (0, (i // tiles_per_dir) * HY_ORDER + o)),
            small((1, d)),
        ],
        out_specs=pl.BlockSpec((None, FILT_TM, d), lambda i, o: (o, i, 0)),
        compiler_params=_cparams("arbitrary", "arbitrary"),
        name="hyena_filter",
    )(z_full, t_full, keep, w1p, row(f_b1), f_w2.astype(F32), row(f_b2), f_w3.astype(F32), row(f_b3),
      row(f_freq), f_w4.astype(F32), deltas)


def _dft_tables():
    n = FFT_N1 * FFT_N2
    k1 = np.arange(FFT_K1)
    n1 = np.arange(FFT_N1)
    ang1 = 2.0 * np.pi * ((k1[:, None] * n1[None, :]) % FFT_N1) / FFT_N1
    f1 = np.zeros((2 * FFT_K1P, FFT_N1), np.float32)
    f1[:FFT_K1] = np.cos(ang1)
    f1[FFT_K1P:FFT_K1P + FFT_K1] = -np.sin(ang1)
    coef = np.where((k1 == 0) | (k1 == FFT_N1 // 2), 1.0, 2.0) / n
    g = np.zeros((FFT_N1 // 2, 2 * FFT_K1P), np.float32)
    ang1h = ang1[:, :FFT_N1 // 2].T
    g[:, :FFT_K1] = np.cos(ang1h) * coef
    g[:, FFT_K1P:FFT_K1P + FFT_K1] = -np.sin(ang1h) * coef
    n2 = np.arange(FFT_N2)
    k2 = np.arange(FFT_N2)
    mf = np.zeros((FFT_K1P, 2 * FFT_N2, 2 * FFT_N2), np.float32)
    mi = np.zeros((FFT_K1P, 2 * FFT_N2, 2 * FFT_N2), np.float32)
    for a in range(FFT_K1):
        k = a + FFT_N1 * k2
        ang = 2.0 * np.pi * ((k[:, None] * n2[None, :]) % n) / n
        er, ei = np.cos(ang), -np.sin(ang)
        mf[a] = np.block([[er, -ei], [ei, er]])
        mi[a] = np.block([[er.T, ei.T], [-ei.T, er.T]])
    return f1, g, mf, mi


def _dft_stage1_kernel(x_ref, f_ref, re_ref, im_ref):
    n1 = x_ref.shape[0] // FFT_N2
    f = f_ref[...]

    def body(g, carry):
        base = g * FFT_NB
        rhs = jnp.concatenate([x_ref[pl.ds(base + j, n1, stride=FFT_N2), :] for j in range(FFT_NB)], axis=1)
        y = _dot(f, rhs.astype(BF16))
        for j in range(FFT_NB):
            yj = y[:, j * LANES:(j + 1) * LANES]
            re_ref[pl.ds(base + j, FFT_K1P, stride=FFT_N2), :] = yj[:FFT_K1P]
            im_ref[pl.ds(base + j, FFT_K1P, stride=FFT_N2), :] = yj[FFT_K1P:]
        return carry

    lax.fori_loop(0, FFT_N2 // FFT_NB, body, 0, unroll=2)


def _dft_stage1(x4, lead, f1):
    _, b, rows, d = x4.shape
    spec_o = pl.BlockSpec((None, FFT_K1P * FFT_N2, LANES), lambda i, c: (i, 0, c))
    return pl.pallas_call(
        _dft_stage1_kernel,
        out_shape=(jax.ShapeDtypeStruct((b, FFT_K1P * FFT_N2, d), F32),) * 2,
        grid=(b, d // LANES),
        in_specs=[pl.BlockSpec((None, None, rows, LANES), lambda i, c: (lead, i, 0, c)),
                  pl.BlockSpec(f1.shape, lambda i, c: (0, 0))],
        out_specs=(spec_o, spec_o),
        compiler_params=_cparams("arbitrary", "arbitrary"),
        name="dft_stage1",
    )(x4, f1)


def _dft_stage2_kernel(ar_ref, ai_ref, mf_ref, xr_ref, xi_ref):
    a = jnp.concatenate([ar_ref[...], ai_ref[...]], axis=0).astype(BF16)
    x = _dot(mf_ref[...], a)
    xr_ref[...] = x[:FFT_N2]
    xi_ref[...] = x[FFT_N2:]


def _dft_stage2(ar, ai, mf):
    b, _, n2, d = ar.shape
    spec = pl.BlockSpec((None, None, n2, d), lambda k, i: (i, k, 0, 0))
    return pl.pallas_call(
        _dft_stage2_kernel,
        out_shape=(jax.ShapeDtypeStruct(ar.shape, F32),) * 2,
        grid=(FFT_K1P, b),
        in_specs=[spec, spec, pl.BlockSpec((None, 2 * n2, 2 * n2), lambda k, i: (k, 0, 0))],
        out_specs=(spec, spec),
        compiler_params=_cparams("arbitrary", "arbitrary"),
        name="dft_stage2",
    )(ar, ai, mf)


def _spectral_kernel(ar_ref, ai_ref, hr_ref, hi_ref, mf_ref, mi_ref, br_ref, bi_ref):
    a = jnp.concatenate([ar_ref[...], ai_ref[...]], axis=0).astype(BF16)
    x = _dot(mf_ref[...], a)
    xr, xi = x[:FFT_N2], x[FFT_N2:]
    hr, hi = hr_ref[...], hi_ref[...]
    y = jnp.concatenate([xr * hr - xi * hi, xr * hi + xi * hr], axis=0).astype(BF16)
    bv = _dot(mi_ref[...], y)
    br_ref[...] = bv[:FFT_N2]
    bi_ref[...] = bv[FFT_N2:]


def _spectral_multiply(ar, ai, hr, hi, order, mf, mi):
    b, _, n2, d = ar.shape
    spec = pl.BlockSpec((None, None, n2, d), lambda k, i: (i, k, 0, 0))
    spec_h = pl.BlockSpec((None, None, n2, d), lambda k, i: (order, k, 0, 0))
    spec_m = pl.BlockSpec((None, 2 * n2, 2 * n2), lambda k, i: (k, 0, 0))
    return pl.pallas_call(
        _spectral_kernel,
        out_shape=(jax.ShapeDtypeStruct(ar.shape, F32),) * 2,
        grid=(FFT_K1P, b),
        in_specs=[spec, spec, spec_h, spec_h, spec_m, spec_m],
        out_specs=(spec, spec),
        compiler_params=_cparams("arbitrary", "arbitrary"),
        name="dft_spectral",
    )(ar, ai, hr, hi, mf, mi)


def _idft_gate_kernel(br_ref, bi_ref, gr_ref, gi_ref, z_ref, gate_ref, bias_ref, o_ref):
    n1h = z_ref.shape[0] // FFT_N2
    gr, gi = gr_ref[...], gi_ref[...]
    bias = bias_ref[...]

    def body(g, carry):
        base = g * FFT_NB
        rows = lambda ref, j, n: ref[pl.ds(base + j, n, stride=FFT_N2), :]
        br = jnp.concatenate([rows(br_ref, j, FFT_K1P) for j in range(FFT_NB)], axis=1).astype(BF16)
        bi = jnp.concatenate([rows(bi_ref, j, FFT_K1P) for j in range(FFT_NB)], axis=1).astype(BF16)
        y = _dot(gr, br) + _dot(gi, bi)
        for j in range(FFT_NB):
            yj = y[:, j * LANES:(j + 1) * LANES]
            o_ref[pl.ds(base + j, n1h, stride=FFT_N2), :] = (
                rows(gate_ref, j, n1h) * (yj + rows(z_ref, j, n1h) * bias))
        return carry

    lax.fori_loop(0, FFT_N2 // FFT_NB, body, 0, unroll=2)


def _idft_gate(br, bi, gr, gi, z4, z_lead, gate4, gate_lead, bias):
    _, b, seq, d = z4.shape
    spec_b = pl.BlockSpec((None, FFT_K1P * FFT_N2, LANES), lambda i, c: (i, 0, c))
    spec_t = pl.BlockSpec(gr.shape, lambda i, c: (0, 0))
    lead_spec = lambda lead: pl.BlockSpec((None, None, seq, LANES), lambda i, c: (lead, i, 0, c))
    return pl.pallas_call(
        _idft_gate_kernel,
        out_shape=jax.ShapeDtypeStruct((1, b, seq, d), F32),
        grid=(b, d // LANES),
        in_specs=[spec_b, spec_b, spec_t, spec_t, lead_spec(z_lead), lead_spec(gate_lead),
                  pl.BlockSpec((1, LANES), lambda i, c: (0, c))],
        out_specs=lead_spec(0),
        compiler_params=_cparams("arbitrary", "arbitrary"),
        name="idft_gate",
    )(br, bi, gr, gi, z4, gate4, bias)


def _hyena_mixer(x2d, mod, batch, seq, w_in, conv_w, conv_b, f_w1, f_b1, f_w2, f_b2, f_w3, f_b3, f_w4,
                 f_freq, bias_d):
    d = x2d.shape[1]
    assert 2 * seq == FFT_N1 * FFT_N2
    n1h = FFT_N1 // 2
    w3 = w_in.reshape(d, 3, d).transpose(1, 0, 2).astype(BF16)
    u = _hy_inproj(x2d, mod, w3, seq)
    cw = jnp.pad(conv_w.reshape(3, 3, d).transpose(1, 0, 2), ((0, 0), (0, SUBLANES - 3), (0, 0)))
    cb = conv_b.reshape(3, 1, d)
    u4 = _short_conv(u.reshape(3, batch, seq, d), cw, cb)
    f1, g, mf, mi = _dft_tables()
    f1 = jnp.asarray(f1).astype(BF16)
    g = jnp.asarray(g).astype(BF16)
    gr, gi = g[:, :FFT_K1P], g[:, FFT_K1P:]
    mf = jnp.asarray(mf).astype(BF16)
    mi = jnp.asarray(mi).astype(BF16)
    rows4 = lambda a: a.reshape(a.shape[0], FFT_K1P, FFT_N2, d)
    rows3 = lambda a: a.reshape(a.shape[0], FFT_K1P * FFT_N2, d)
    kern = _hyena_filter(seq, d, f_w1, f_b1, f_w2, f_b2, f_w3, f_b3, f_w4, f_freq)
    kr, ki = _dft_stage1(kern[None], 0, f1)
    hr, hi = _dft_stage2(rows4(kr), rows4(ki), mf)
    z4, z_lead = u4, 0
    f1h = f1[:, :n1h]
    for o in range(HY_ORDER):
        ar, ai = _dft_stage1(z4, z_lead, f1h)
        br, bi = _spectral_multiply(rows4(ar), rows4(ai), hr, hi, o, mf, mi)
        z4 = _idft_gate(rows3(br), rows3(bi), gr, gi, z4, z_lead, u4, 1 + o,
                        bias_d[o].astype(F32).reshape(1, d))
        z_lead = 0
    return z4.reshape(batch * seq, d)


def _mod_table(m_layer, d):
    m6 = m_layer.reshape(m_layer.shape[0], 6, d)
    return jnp.pad(m6, ((0, 0), (0, SUBLANES - 6), (0, 0)))


def kernel(x, c, ctx, c_ctx, ada_w, ada_b, ln_g, ln_b, attn_w_in, attn_w_out, attn_sink, attn_q_gain, attn_k_gain, hy_w_in, hy_conv_w, hy_conv_b, hy_f_w1, hy_f_b1, hy_f_w2, hy_f_b2, hy_f_w3, hy_f_b3, hy_f_w4, hy_f_freq, hy_bias_d, hy_w_out, moe_router_w, moe_router_b, moe_w_gate, moe_w_up, moe_w_down, moe_s_gate, moe_s_up, moe_s_down):
    batch, seq, d = x.shape
    n_ctx = ctx.shape[1]
    depth = ada_w.shape[0]
    assert depth == DEPTH and HEAD_DIM == LANES and batch + 1 <= SUBLANES
    cc = jnp.concatenate([c, c_ctx[None, :], jnp.zeros((SUBLANES - batch - 1, d), c.dtype)], axis=0)
    m_all = _ada_modulation(cc.astype(F32), ada_w, ada_b)
    x2d = x.reshape(batch * seq, d)
    cos_t, sin_t = _rope_tables(seq)
    s_gate, s_up, s_down = moe_s_gate.astype(BF16), moe_s_up.astype(BF16), moe_s_down.astype(BF16)
    for l in range(depth):
        mod = _mod_table(m_all[l], d)
        i = l // 2
        router_wt = moe_router_w[l].astype(F32).T
        if l % 2 == 0:
            w_in = attn_w_in[i].astype(BF16)
            qkv = _attn_inproj(x2d, mod, w_in, cos_t, sin_t, attn_q_gain[i], attn_k_gain[i],
                               seq, 0, INPROJ_TM)
            ones = jnp.ones((n_ctx, HEAD_DIM), F32)
            qkv_ctx = _attn_inproj(ctx.reshape(batch * n_ctx, d), mod, w_in, ones, jnp.zeros_like(ones),
                                   attn_q_gain[i], attn_k_gain[i], batch * n_ctx, batch, n_ctx)
            qkv = qkv.reshape(batch, seq, -1)
            qkv_ctx = qkv_ctx.reshape(batch, n_ctx, -1)
            o_a = _win_attention(qkv, qkv_ctx, attn_sink[i].astype(F32))
            o_b = _dense_attention(qkv, qkv_ctx)
            w_out = attn_w_out[i].astype(BF16)
            na = A_Q_HEADS * HEAD_DIM
            acts = [o_a.reshape(batch * seq, na), o_b.reshape(batch * seq, -1)]
            weights = [w_out[:na], w_out[na:]]
        else:
            z = _hyena_mixer(x2d, mod, batch, seq, hy_w_in[i], hy_conv_w[i], hy_conv_b[i], hy_f_w1[i],
                             hy_f_b1[i], hy_f_w2[i], hy_f_b2[i], hy_f_w3[i], hy_f_b3[i], hy_f_w4[i],
                             hy_f_freq[i], hy_bias_d[i])
            acts = [z]
            weights = [hy_w_out[i].astype(BF16)]
        x2d, hp, logits_t = _outproj_ln(acts, weights, x2d, mod, ln_g[l, 0], ln_b[l, 0], router_wt, seq)
        x2d = _moe_layer(x2d, hp, logits_t, mod, seq, l, moe_router_b[l], moe_w_gate, moe_w_up, moe_w_down,
                         s_gate, s_up, s_down, ln_g[l, 1], ln_b[l, 1])
    return x2d.reshape(batch, seq, d)
```

```python
import functools
import math

import numpy as np
import jax
import jax.numpy as jnp
from jax import lax
from jax.experimental import pallas as pl
from jax.experimental.pallas import tpu as pltpu

F32 = jnp.float32
BF16 = jnp.bfloat16

DEPTH = 2
GRID_W = 64
HEAD_DIM = 128
A_Q_HEADS, A_KV_HEADS, B_Q_HEADS, B_KV_HEADS = 8, 2, 8, 2
GQA_GROUP = A_Q_HEADS // A_KV_HEADS
WINDOW = 128
Q_BLOCK = 128
ROPE_THETA = 10000.0
QK_NORM_EPS = 1e-6
LN_EPS = 1e-5
HY_ORDER = 2
HY_EMB = 33
HY_FILTER_WIDTH = 64
HY_DECAY_FAST, HY_DECAY_SLOW, HY_DECAY_TARGET = 0.3, 1.5, 1e-2
N_EXPERTS = 64
TOP_K = 8
N_GROUPS = 8
TOP_GROUPS = 4
ROUTED_SCALE = 2.5
DEEPNORM_ALPHA = (2 * DEPTH) ** 0.25

LANES = 128
SUBLANES = 8
VMEM_LIMIT = 56 * 1024 * 1024

ADA_TN = 1024
INPROJ_TM = 512
DENSE_TQ = 256
DENSE_TK = 1024
OUTPROJ_TM = 512
OUTPROJ_CHUNK = 128
HY_TM = 512
CONV_TC = 512
MOE_TM = 512
MOE_CHAINS = 1
ROUTE_TT = 512
DISP_TM = 256
COMB_TM = 256
FFT_N1, FFT_N2 = 64, 128
FFT_K1 = FFT_N1 // 2 + 1
FFT_K1P = 40
FFT_NB = 8
FILT_TM = 512


def _cparams(*sem):
    return pltpu.CompilerParams(dimension_semantics=sem, vmem_limit_bytes=VMEM_LIMIT)


def _silu(v):
    return v * jax.nn.sigmoid(v)


def _split_bf16(v):
    hi = v.astype(BF16)
    lo = (v - hi.astype(F32)).astype(BF16)
    return hi, lo


def _dot(a, b):
    return jnp.dot(a, b, preferred_element_type=F32)


def _dot_nt(a, b):
    return lax.dot_general(a, b, (((1,), (1,)), ((), ())), preferred_element_type=F32)


def _pack_bf16_pairs(v):
    n = v.shape[1] // 2
    bits = pltpu.bitcast(v.astype(BF16).astype(F32), jnp.uint32)
    return (bits[:, :n] >> 16) | (bits[:, n:] & jnp.uint32(0xFFFF0000))


def _unpack_bf16_pairs(p):
    lo = pltpu.bitcast(p << 16, F32)
    hi = pltpu.bitcast(p & jnp.uint32(0xFFFF0000), F32)
    return lo, hi


def _dot3(a, b):
    ah, al = _split_bf16(a)
    bh, bl = _split_bf16(b)
    return _dot(ah, bh) + _dot(al, bh) + _dot(ah, bl)


def _ada_kernel(c_ref, w_ref, b_ref, o_ref):
    s = _silu(c_ref[...]).astype(BF16)
    o_ref[...] = _dot(s, w_ref[...].astype(BF16)) + b_ref[...]


def _ada_modulation(cc, ada_w, ada_b):
    depth, d, n = ada_w.shape
    rows = cc.shape[0]
    return pl.pallas_call(
        _ada_kernel,
        out_shape=jax.ShapeDtypeStruct((depth, rows, n), F32),
        grid=(depth, n // ADA_TN),
        in_specs=[
            pl.BlockSpec((rows, d), lambda l, j: (0, 0)),
            pl.BlockSpec((None, d, ADA_TN), lambda l, j: (l, 0, j)),
            pl.BlockSpec((None, 1, ADA_TN), lambda l, j: (l, 0, j)),
        ],
        out_specs=pl.BlockSpec((None, rows, ADA_TN), lambda l, j: (l, 0, j)),
        compiler_params=_cparams("arbitrary", "arbitrary"),
        name="ada_modulation",
    )(cc, ada_w, ada_b.reshape(depth, 1, n))


_N_SLOTS = (A_Q_HEADS + 2 * A_KV_HEADS + B_Q_HEADS + 2 * B_KV_HEADS)
_QA0, _KA0, _VA0 = 0, A_Q_HEADS, A_Q_HEADS + A_KV_HEADS
_QB0 = A_Q_HEADS + 2 * A_KV_HEADS
_KB0, _VB0 = _QB0 + B_Q_HEADS, _QB0 + B_Q_HEADS + B_KV_HEADS


def _slot_kind(j):
    if j < _KA0:
        return "qa"
    if j < _VA0:
        return "ka"
    if j < _QB0:
        return "v"
    if j < _KB0:
        return "qb"
    if j < _VB0:
        return "kb"
    return "v"


def _attn_inproj_kernel(x_ref, mod_ref, w_ref, cos_ref, sin_ref, qg_ref, kg_ref, o_ref):
    x = x_ref[...]
    h = (x * (1.0 + mod_ref[1:2, :]) + mod_ref[0:1, :]).astype(BF16)
    y = _dot(h, w_ref[...])
    cos = cos_ref[...]
    sin = sin_ref[...]
    lane = lax.broadcasted_iota(jnp.int32, cos.shape, 1)
    low_half = (lane & (HEAD_DIM // 2 - 1)) < (HEAD_DIM // 4)
    q_scale = HEAD_DIM ** -0.5
    for j in range(_N_SLOTS):
        kind = _slot_kind(j)
        blk = y[:, j * HEAD_DIM:(j + 1) * HEAD_DIM]
        if kind in ("qb", "kb"):
            gain = qg_ref[...] if kind == "qb" else kg_ref[...]
            ms = jnp.mean(blk * blk, axis=-1, keepdims=True)
            blk = blk * lax.rsqrt(ms + QK_NORM_EPS) * gain
        if kind != "v":
            partner = jnp.where(low_half,
                                pltpu.roll(blk, HEAD_DIM - HEAD_DIM // 4, 1),
                                pltpu.roll(blk, HEAD_DIM // 4, 1))
            blk = blk * cos + partner * sin
        if kind in ("qa", "qb"):
            blk = blk * q_scale
        o_ref[:, j * HEAD_DIM:(j + 1) * HEAD_DIM] = blk.astype(o_ref.dtype)


def _attn_inproj(x2d, mod, w_bf16, cos_t, sin_t, q_gain, k_gain, rows_per_mod, mod_base, tm):
    rows, d = x2d.shape
    n = w_bf16.shape[1]
    tiles_per_mod = rows_per_mod // tm
    tab_tiles = cos_t.shape[0] // tm
    return pl.pallas_call(
        _attn_inproj_kernel,
        out_shape=jax.ShapeDtypeStruct((rows, n), BF16),
        grid=(rows // tm,),
        in_specs=[
            pl.BlockSpec((tm, d), lambda i: (i, 0)),
            pl.BlockSpec((None, SUBLANES, d), lambda i: (mod_base + i // tiles_per_mod, 0, 0)),
            pl.BlockSpec((d, n), lambda i: (0, 0), pipeline_mode=pl.Buffered(1)),
            pl.BlockSpec((tm, HEAD_DIM), lambda i: (i % tab_tiles, 0)),
            pl.BlockSpec((tm, HEAD_DIM), lambda i: (i % tab_tiles, 0)),
            pl.BlockSpec((1, HEAD_DIM), lambda i: (0, 0)),
            pl.BlockSpec((1, HEAD_DIM), lambda i: (0, 0)),
        ],
        out_specs=pl.BlockSpec((tm, n), lambda i: (i, 0)),
        compiler_params=_cparams("arbitrary"),
        name="attn_inproj",
    )(x2d, mod, w_bf16, cos_t, sin_t, q_gain.reshape(1, -1), k_gain.reshape(1, -1))


def _rope_tables(seq):
    rows = seq // GRID_W
    row = jnp.repeat(jnp.arange(rows, dtype=F32), GRID_W)
    col = jnp.tile(jnp.arange(GRID_W, dtype=F32), rows)
    n_freq = HEAD_DIM // 4
    inv = ROPE_THETA ** (-jnp.arange(n_freq, dtype=F32) / n_freq)
    ang_r, ang_c = row[:, None] * inv, col[:, None] * inv
    cos = jnp.concatenate([jnp.cos(ang_r)] * 2 + [jnp.cos(ang_c)] * 2, axis=-1)
    sin = jnp.concatenate([-jnp.sin(ang_r), jnp.sin(ang_r), -jnp.sin(ang_c), jnp.sin(ang_c)], axis=-1)
    return cos, sin


def _win_attn_kernel(sink_ref, q_ref, k_ref, v_ref, kc_ref, vc_ref, o_ref):
    seq = q_ref.shape[0]
    hk = pl.program_id(1)
    n_loc = 3 * Q_BLOCK
    rows = GQA_GROUP * Q_BLOCK
    row = lax.broadcasted_iota(jnp.int32, (rows, 1), 0)
    col = lax.broadcasted_iota(jnp.int32, (1, n_loc), 1)
    sink_col = jnp.full((rows, 1), sink_ref[hk * GQA_GROUP + GQA_GROUP - 1], F32)
    for g in range(GQA_GROUP - 2, -1, -1):
        sink_col = jnp.where(row < (g + 1) * Q_BLOCK, sink_ref[hk * GQA_GROUP + g], sink_col)
    kc = kc_ref[...]
    vc = vc_ref[...]

    def body(n, carry):
        q0 = pl.multiple_of(n * Q_BLOCK, Q_BLOCK)
        start = pl.multiple_of(jnp.clip(q0 - Q_BLOCK, 0, seq - n_loc), Q_BLOCK)
        q = jnp.concatenate(
            [q_ref[pl.ds(q0, Q_BLOCK), g * HEAD_DIM:(g + 1) * HEAD_DIM] for g in range(GQA_GROUP)], axis=0)
        kw = k_ref[pl.ds(start, n_loc), :]
        vw = v_ref[pl.ds(start, n_loc), :]
        s_loc = _dot_nt(q, kw)
        s_ctx = _dot_nt(q, kc)
        qpos = q0 + (row & (Q_BLOCK - 1))
        kpos = start + col
        s_loc = jnp.where(jnp.abs(qpos - kpos) <= WINDOW, s_loc, -jnp.inf)
        m = jnp.maximum(jnp.maximum(jnp.max(s_loc, -1, keepdims=True), jnp.max(s_ctx, -1, keepdims=True)),
                        sink_col)
        p_loc = jnp.exp(s_loc - m)
        p_ctx = jnp.exp(s_ctx - m)
        denom = (jnp.sum(p_loc, -1, keepdims=True) + jnp.sum(p_ctx, -1, keepdims=True)
                 + jnp.exp(sink_col - m))
        o = (_dot(p_loc.astype(BF16), vw) + _dot(p_ctx.astype(BF16), vc)) / denom
        for g in range(GQA_GROUP):
            o_ref[pl.ds(q0, Q_BLOCK), g * HEAD_DIM:(g + 1) * HEAD_DIM] = (
                o[g * Q_BLOCK:(g + 1) * Q_BLOCK].astype(o_ref.dtype))
        return carry

    lax.fori_loop(0, seq // Q_BLOCK, body, 0)


def _win_attention(qkv, qkv_ctx, sink):
    b, s, _ = qkv.shape
    c = qkv_ctx.shape[1]
    gw = GQA_GROUP * HEAD_DIM
    return pl.pallas_call(
        _win_attn_kernel,
        out_shape=jax.ShapeDtypeStruct((b, s, A_Q_HEADS * HEAD_DIM), BF16),
        grid=(b, A_KV_HEADS),
        in_specs=[
            pl.BlockSpec(memory_space=pltpu.SMEM),
            pl.BlockSpec((None, s, gw), lambda i, h: (i, 0, h)),
            pl.BlockSpec((None, s, HEAD_DIM), lambda i, h: (i, 0, _KA0 + h)),
            pl.BlockSpec((None, s, HEAD_DIM), lambda i, h: (i, 0, _VA0 + h)),
            pl.BlockSpec((None, c, HEAD_DIM), lambda i, h: (i, 0, _KA0 + h)),
            pl.BlockSpec((None, c, HEAD_DIM), lambda i, h: (i, 0, _VA0 + h)),
        ],
        out_specs=pl.BlockSpec((None, s, gw), lambda i, h: (i, 0, h)),
        compiler_params=_cparams("arbitrary", "arbitrary"),
        name="win_attention",
    )(sink, qkv, qkv, qkv, qkv_ctx, qkv_ctx)


def _dense_attn_kernel(q_ref, k_ref, v_ref, kc_ref, vc_ref, o_ref, m_sc, l_sc, acc_sc):
    seq = k_ref.shape[0]
    m_sc[...] = jnp.full(m_sc.shape, -jnp.inf, F32)
    l_sc[...] = jnp.zeros(l_sc.shape, F32)
    acc_sc[...] = jnp.zeros(acc_sc.shape, F32)

    def step(kb, vb):
        reps = kb.shape[0] // LANES
        for g in range(GQA_GROUP):
            s = _dot_nt(q_ref[:, g * HEAD_DIM:(g + 1) * HEAD_DIM], kb)
            m_prev = m_sc[g]
            m_new = jnp.maximum(m_prev, jnp.max(s, -1, keepdims=True))
            alpha = jnp.exp(m_prev - m_new)
            p = jnp.exp(s - jnp.concatenate([m_new] * reps, axis=1))
            l_sc[g] = alpha * l_sc[g] + jnp.sum(p, -1, keepdims=True)
            acc_sc[g] = alpha * acc_sc[g] + _dot(p.astype(BF16), vb)
            m_sc[g] = m_new

    def body(j, carry):
        k0 = pl.multiple_of(j * DENSE_TK, DENSE_TK)
        step(k_ref[pl.ds(k0, DENSE_TK), :], v_ref[pl.ds(k0, DENSE_TK), :])
        return carry

    lax.fori_loop(0, seq // DENSE_TK, body, 0)
    step(kc_ref[...], vc_ref[...])
    for g in range(GQA_GROUP):
        o_ref[:, g * HEAD_DIM:(g + 1) * HEAD_DIM] = (acc_sc[g] / l_sc[g]).astype(o_ref.dtype)


def _dense_attention(qkv, qkv_ctx):
    b, s, _ = qkv.shape
    c = qkv_ctx.shape[1]
    gw = GQA_GROUP * HEAD_DIM
    qb_blk = _QB0 * HEAD_DIM // gw
    assert s % DENSE_TK == 0 and s % DENSE_TQ == 0
    stat = pltpu.VMEM((GQA_GROUP, DENSE_TQ, LANES), F32)
    return pl.pallas_call(
        _dense_attn_kernel,
        out_shape=jax.ShapeDtypeStruct((b, s, B_Q_HEADS * HEAD_DIM), BF16),
        grid=(b, B_KV_HEADS, s // DENSE_TQ),
        in_specs=[
            pl.BlockSpec((None, DENSE_TQ, gw), lambda i, h, t: (i, t, qb_blk + h)),
            pl.BlockSpec((None, s, HEAD_DIM), lambda i, h, t: (i, 0, _KB0 + h)),
            pl.BlockSpec((None, s, HEAD_DIM), lambda i, h, t: (i, 0, _VB0 + h)),
            pl.BlockSpec((None, c, HEAD_DIM), lambda i, h, t: (i, 0, _KB0 + h)),
            pl.BlockSpec((None, c, HEAD_DIM), lambda i, h, t: (i, 0, _VB0 + h)),
        ],
        out_specs=pl.BlockSpec((None, DENSE_TQ, gw), lambda i, h, t: (i, t, h)),
        scratch_shapes=[stat, stat, stat],
        compiler_params=_cparams("arbitrary", "arbitrary", "arbitrary"),
        name="dense_attention",
    )(qkv, qkv, qkv, qkv_ctx, qkv_ctx)


def _layer_norm(r, g, b):
    mu = jnp.mean(r, axis=-1, keepdims=True)
    c = r - mu
    var = jnp.mean(c * c, axis=-1, keepdims=True)
    return c * lax.rsqrt(var + LN_EPS) * g + b


def _outproj_ln_kernel(n_in, *refs):
    a_refs = refs[:n_in]
    w_refs = refs[n_in:2 * n_in]
    x_ref, mod_ref, lng_ref, lnb_ref, rwh_ref, rwl_ref, xo_ref, h2_ref, lg_ref = refs[2 * n_in:]
    y = _dot(a_refs[0][...].astype(BF16), w_refs[0][...])
    for a_ref, w_ref in zip(a_refs[1:], w_refs[1:]):
        y = y + _dot(a_ref[...].astype(BF16), w_ref[...])
    rwh, rwl = rwh_ref[...], rwl_ref[...]
    for c in range(y.shape[0] // OUTPROJ_CHUNK):
        sl = slice(c * OUTPROJ_CHUNK, (c + 1) * OUTPROJ_CHUNK)
        r = DEEPNORM_ALPHA * x_ref[sl, :] + mod_ref[2:3, :] * y[sl]
        xn = _layer_norm(r, lng_ref[...], lnb_ref[...])
        xo_ref[sl, :] = xn
        h2 = xn * (1.0 + mod_ref[4:5, :]) + mod_ref[3:4, :]
        h2_ref[sl, :] = _pack_bf16_pairs(h2)
        hi, lo = _split_bf16(h2)
        lg_ref[:, sl] = _dot_nt(rwh, hi) + _dot_nt(rwh, lo) + _dot_nt(rwl, hi)


def _outproj_ln(acts, weights, x2d, mod, ln_g, ln_b, router_wt, rows_per_mod):
    rows, d = x2d.shape
    tm = OUTPROJ_TM
    n_in = len(acts)
    n_e = router_wt.shape[0]
    tiles_per_mod = rows_per_mod // tm
    rwh, rwl = _split_bf16(router_wt)
    in_specs = (
        [pl.BlockSpec((tm, a.shape[1]), lambda i: (i, 0)) for a in acts]
        + [pl.BlockSpec(w.shape, lambda i: (0, 0), pipeline_mode=pl.Buffered(1)) for w in weights]
        + [
            pl.BlockSpec((tm, d), lambda i: (i, 0)),
            pl.BlockSpec((None, SUBLANES, d), lambda i: (i // tiles_per_mod, 0, 0)),
            pl.BlockSpec((1, d), lambda i: (0, 0)),
            pl.BlockSpec((1, d), lambda i: (0, 0)),
            pl.BlockSpec((n_e, d), lambda i: (0, 0)),
            pl.BlockSpec((n_e, d), lambda i: (0, 0)),
        ])
    return pl.pallas_call(
        functools.partial(_outproj_ln_kernel, n_in),
        out_shape=(jax.ShapeDtypeStruct((rows, d), F32), jax.ShapeDtypeStruct((rows, d // 2), jnp.uint32),
                   jax.ShapeDtypeStruct((n_e, rows), F32)),
        grid=(rows // tm,),
        in_specs=in_specs,
        out_specs=(pl.BlockSpec((tm, d), lambda i: (i, 0)), pl.BlockSpec((tm, d // 2), lambda i: (i, 0)),
                   pl.BlockSpec((n_e, tm), lambda i: (0, i))),
        compiler_params=_cparams("arbitrary"),
        name="outproj_ln",
    )(*acts, *weights, x2d, mod, ln_g.reshape(1, d), ln_b.reshape(1, d), rwh, rwl)


def _route_kernel(lg_ref, b_ref, tri_ref, e_ref, w_ref, r_ref, cnt_ref, carry_sc):
    @pl.when(pl.program_id(0) == 0)
    def _():
        carry_sc[...] = jnp.zeros(carry_sc.shape, F32)

    tt = lg_ref.shape[1]
    gsz = N_EXPERTS // N_GROUPS
    s = jax.nn.sigmoid(lg_ref[...])
    biased = s + b_ref[...]
    sub = lax.broadcasted_iota(jnp.int32, (gsz, tt), 0).astype(F32)
    rows = []
    for g in range(N_GROUPS):
        v = biased[g * gsz:(g + 1) * gsz]
        m1 = jnp.max(v, axis=0, keepdims=True)
        first = jnp.min(jnp.where(v == m1, sub, float(gsz)), axis=0, keepdims=True)
        m2 = jnp.max(jnp.where(sub == first, -jnp.inf, v), axis=0, keepdims=True)
        rows.append(m1 + m2)
    gs = jnp.concatenate(rows, axis=0)
    gidx = lax.broadcasted_iota(jnp.int32, (N_GROUPS, tt), 0)
    grank = jnp.zeros((N_GROUPS, tt), F32)
    for g2 in range(N_GROUPS):
        row = gs[g2:g2 + 1]
        ahead = jnp.where(row > gs, 1.0, jnp.where(row == gs, jnp.where(gidx > g2, 1.0, 0.0), 0.0))
        grank = grank + ahead
    gkeep = jnp.where(grank < TOP_GROUPS, 1.0, 0.0)
    keep = jnp.concatenate([jnp.broadcast_to(gkeep[g:g + 1], (gsz, tt)) for g in range(N_GROUPS)], axis=0)
    cur = jnp.where(keep > 0.5, biased, -jnp.inf)
    eidx = lax.broadcasted_iota(jnp.int32, (N_EXPERTS, tt), 0).astype(F32)
    chosen = jnp.zeros((N_EXPERTS, tt), F32)
    e_rows, s_rows = [], []
    for _ in range(TOP_K):
        m = jnp.max(cur, axis=0, keepdims=True)
        idx = jnp.min(jnp.where(cur == m, eidx, float(N_EXPERTS)), axis=0, keepdims=True)
        hit = eidx == idx
        s_rows.append(jnp.sum(jnp.where(hit, s, 0.0), axis=0, keepdims=True))
        e_rows.append(idx)
        chosen = jnp.where(hit, 1.0, chosen)
        cur = jnp.where(hit, -jnp.inf, cur)
    top_e = jnp.concatenate(e_rows, axis=0)
    sc = jnp.concatenate(s_rows, axis=0)
    w_ref[...] = sc / jnp.sum(sc, axis=0, keepdims=True) * ROUTED_SCALE
    e_ref[...] = top_e.astype(jnp.int32)
    carry = carry_sc[...]
    cnt = _dot(chosen.astype(BF16), tri_ref[...]) + jnp.concatenate([carry] * (tt // LANES), axis=1)
    rank = jnp.zeros((TOP_K, tt), F32)
    for e in range(N_EXPERTS):
        rank = rank + jnp.where(top_e == float(e), cnt[e:e + 1], 0.0)
    r_ref[...] = rank.astype(jnp.int32)
    carry = carry + jnp.sum(chosen, axis=1, keepdims=True)
    carry_sc[...] = carry
    cnt_ref[...] = carry


def _route(logits_t, router_b):
    n_e, t = logits_t.shape
    tt = ROUTE_TT
    tri = (jnp.arange(tt)[:, None] < jnp.arange(tt)[None, :]).astype(BF16)
    tok_spec = pl.BlockSpec((TOP_K, tt), lambda i: (0, i))
    return pl.pallas_call(
        _route_kernel,
        out_shape=(jax.ShapeDtypeStruct((TOP_K, t), jnp.int32), jax.ShapeDtypeStruct((TOP_K, t), F32),
                   jax.ShapeDtypeStruct((TOP_K, t), jnp.int32), jax.ShapeDtypeStruct((n_e, LANES), F32)),
        grid=(t // tt,),
        in_specs=[pl.BlockSpec((n_e, tt), lambda i: (0, i)),
                  pl.BlockSpec((n_e, 1), lambda i: (0, 0)),
                  pl.BlockSpec((tt, tt), lambda i: (0, 0))],
        out_specs=(tok_spec, tok_spec, tok_spec, pl.BlockSpec((n_e, LANES), lambda i: (0, 0))),
        scratch_shapes=[pltpu.VMEM((n_e, LANES), F32)],
        compiler_params=_cparams("arbitrary"),
        name="moe_route",
    )(logits_t, router_b.astype(F32).reshape(n_e, 1), tri)


def _routing_plan(top_e, rank, counts_f):
    t = top_e.shape[1]
    counts = counts_f[:, 0].astype(jnp.int32)
    pcounts = (counts + MOE_TM - 1) // MOE_TM * MOE_TM
    pends = jnp.cumsum(pcounts)
    pstarts = pends - pcounts
    first_row = jnp.sum(jnp.where(top_e[..., None] == jnp.arange(N_EXPERTS, dtype=jnp.int32), pstarts, 0), axis=-1)
    dest = (first_row + rank).reshape(-1)
    n_blocks = t * TOP_K // MOE_TM + N_EXPERTS
    n_used = (pends[-1] // MOE_TM).astype(jnp.int32)
    blk_start = jnp.minimum(jnp.arange(n_blocks, dtype=jnp.int32), n_used - 1) * MOE_TM
    blk_e = jnp.sum((pends[None, :] <= blk_start[:, None]).astype(jnp.int32), axis=1)
    blk_e = jnp.minimum(blk_e, N_EXPERTS - 1).astype(jnp.int32)
    return dest, blk_e, n_used.reshape(1), (pstarts + counts).astype(jnp.int32), (pcounts - counts).astype(jnp.int32)


_PAD_PIECES = tuple(1 << b for b in range(MOE_TM.bit_length() - 2, SUBLANES.bit_length() - 2, -1))


def _dispatch_kernel(dest_ref, pad_start, pad_len, n_used, hp_ref, xs_hbm, zbuf, sem, zsem):
    i = pl.program_id(0)
    tm = hp_ref.shape[0]
    t_total = pl.num_programs(0) * tm
    n_blocks = xs_hbm.shape[0] // MOE_TM

    def zero_copy(row0, n):
        return pltpu.make_async_copy(zbuf.at[pl.ds(0, n), :], xs_hbm.at[pl.ds(row0, n), :], zsem)

    def for_each_pad_row(fn):
        def per_expert(e, carry):
            start, length = pad_start[e], pad_len[e]
            head = jnp.minimum((-start) & (SUBLANES - 1), length)

            def per_row(j, c):
                fn(zero_copy(start + j, 1))
                return c
            lax.fori_loop(0, head, per_row, 0)
            rest = length - head
            for piece in _PAD_PIECES:
                @pl.when((rest & piece) != 0)
                def _():
                    fn(zero_copy(pl.multiple_of(start + head + (rest & ~(2 * piece - 1)), SUBLANES), piece))
            return carry
        lax.fori_loop(0, N_EXPERTS, per_expert, 0)

        def per_block(blk, carry):
            fn(zero_copy(pl.multiple_of(blk * MOE_TM, MOE_TM), MOE_TM))
            return carry
        lax.fori_loop(n_used[0], n_blocks, per_block, 0)

    @pl.when(i == 0)
    def _():
        zbuf[...] = jnp.zeros(zbuf.shape, zbuf.dtype)
        for_each_pad_row(lambda cp: cp.start())

    def issue(r, carry):
        src = hp_ref.at[pl.ds(r, 1), :]
        for k in range(TOP_K):
            d = dest_ref[k * t_total + i * tm + r]
            pltpu.make_async_copy(src, xs_hbm.at[pl.ds(d, 1), :], sem).start(priority=k % 2)
        return carry

    lax.fori_loop(0, tm, issue, 0)
    for k in range(TOP_K):
        pltpu.make_async_copy(hp_ref, xs_hbm.at[pl.ds(0, tm), :], sem).wait()

    @pl.when(i == 0)
    def _():
        for_each_pad_row(lambda cp: cp.wait())


def _moe_dispatch(hp, dest, pad_start, pad_len, n_used, n_rows):
    t, dh = hp.shape
    tm = DISP_TM
    grid_spec = pltpu.PrefetchScalarGridSpec(
        num_scalar_prefetch=4,
        grid=(t // tm,),
        in_specs=[pl.BlockSpec((tm, dh), lambda i, ds_, ps, pn, nu: (i, 0))],
        out_specs=pl.BlockSpec(memory_space=pl.ANY),
        scratch_shapes=[pltpu.VMEM((MOE_TM, dh), jnp.uint32), pltpu.SemaphoreType.DMA,
                        pltpu.SemaphoreType.DMA],
    )
    return pl.pallas_call(
        _dispatch_kernel,
        out_shape=jax.ShapeDtypeStruct((n_rows, dh), jnp.uint32),
        grid_spec=grid_spec,
        compiler_params=_cparams("arbitrary"),
        name="moe_dispatch",
    )(dest, pad_start, pad_len, n_used, hp)


def _moe_ffn_kernel(blk_e, n_used, xs_ref, wg_ref, wu_ref, wd_ref, y_ref, wgb, wub, wdb):
    i = pl.program_id(0)

    @pl.when(i < n_used[0])
    def _():
        @pl.when((i == 0) | (blk_e[i] != blk_e[jnp.maximum(i - 1, 0)]))
        def _():
            wgb[...] = wg_ref[...].astype(BF16)
            wub[...] = wu_ref[...].astype(BF16)
            wdb[...] = wd_ref[...].astype(BF16)

        rows = xs_ref.shape[0] // MOE_CHAINS
        for c in range(MOE_CHAINS):
            sl = pl.ds(c * rows, rows)
            lo, hi = _unpack_bf16_pairs(xs_ref[sl, :])
            x = jnp.concatenate([lo, hi], axis=1).astype(BF16)
            hid = (_silu(_dot(x, wgb[...])) * _dot(x, wub[...])).astype(BF16)
            y_ref[sl, :] = _pack_bf16_pairs(_dot(hid, wdb[...]))

    @pl.when(i >= n_used[0])
    def _():
        y_ref[...] = jnp.zeros(y_ref.shape, y_ref.dtype)


def _moe_ffn(xs, blk_e, n_used, layer, w_gate, w_up, w_down):
    n_rows, dh = xs.shape
    d = 2 * dh
    n_blocks = blk_e.shape[0]
    ff = w_gate.shape[3]
    tm = MOE_TM
    row_blk = lambda i, be, nu: (jnp.minimum(i, nu[0] - 1), 0)
    grid_spec = pltpu.PrefetchScalarGridSpec(
        num_scalar_prefetch=2,
        grid=(n_blocks,),
        in_specs=[
            pl.BlockSpec((tm, dh), row_blk),
            pl.BlockSpec((None, None, d, ff), lambda i, be, nu: (layer, be[i], 0, 0)),
            pl.BlockSpec((None, None, d, ff), lambda i, be, nu: (layer, be[i], 0, 0)),
            pl.BlockSpec((None, None, ff, d), lambda i, be, nu: (layer, be[i], 0, 0)),
        ],
        out_specs=pl.BlockSpec((tm, dh), lambda i, be, nu: (i, 0)),
        scratch_shapes=[pltpu.VMEM((d, ff), BF16), pltpu.VMEM((d, ff), BF16), pltpu.VMEM((ff, d), BF16)],
    )
    return pl.pallas_call(
        _moe_ffn_kernel,
        out_shape=jax.ShapeDtypeStruct((n_rows, dh), jnp.uint32),
        grid_spec=grid_spec,
        compiler_params=_cparams("arbitrary"),
        name="moe_ffn",
    )(blk_e, n_used, xs, w_gate, w_up, w_down)


def _moe_combine_kernel(dest_ref, yp_hbm, gw_ref, hp_ref, x_ref, mod_ref, sg_ref, su_ref, sd_ref,
                        lng_ref, lnb_ref, o_ref, gbuf, sem):
    i = pl.program_id(0)
    n_tiles = pl.num_programs(0)
    tm = hp_ref.shape[0]
    t_total = n_tiles * tm
    slot = i % 2

    def issue_tile(tile, slot_):
        def issue(r, carry):
            for k in range(TOP_K):
                d = dest_ref[k * t_total + tile * tm + r]
                pltpu.make_async_copy(yp_hbm.at[pl.ds(d, 1), :], gbuf.at[slot_, k, pl.ds(r, 1), :],
                                      sem.at[slot_]).start()
            return carry

        lax.fori_loop(0, tm, issue, 0)

    @pl.when(i == 0)
    def _():
        issue_tile(0, 0)

    for s in range(2):
        @pl.when((i + 1 < n_tiles) & (slot == 1 - s))
        def _():
            issue_tile(i + 1, s)

    lo, hi = _unpack_bf16_pairs(hp_ref[...])
    h = jnp.concatenate([lo, hi], axis=1).astype(BF16)
    hid = (_silu(_dot(h, sg_ref[...])) * _dot(h, su_ref[...])).astype(BF16)
    out = _dot(hid, sd_ref[...])
    half = out.shape[1] // 2
    out_lo, out_hi = out[:, :half], out[:, half:]
    for k in range(TOP_K):
        pltpu.make_async_copy(yp_hbm.at[pl.ds(0, tm), :], gbuf.at[slot, k], sem.at[slot]).wait()
    gw = gw_ref[...]
    for k in range(TOP_K):
        lo, hi = _unpack_bf16_pairs(gbuf[slot, k])
        wk = gw[:, k:k + 1]
        out_lo = out_lo + wk * lo
        out_hi = out_hi + wk * hi
    out = jnp.concatenate([out_lo, out_hi], axis=1)
    r = DEEPNORM_ALPHA * x_ref[...] + mod_ref[5:6, :] * out
    o_ref[...] = _layer_norm(r, lng_ref[...], lnb_ref[...])


def _moe_combine(yp, dest, gw, hp, x2d, mod, layer, s_gate, s_up, s_down, ln_g, ln_b, rows_per_mod):
    t, d = x2d.shape
    tm = COMB_TM
    sf = s_gate.shape[2]
    tiles_per_mod = rows_per_mod // tm
    grid_spec = pltpu.PrefetchScalarGridSpec(
        num_scalar_prefetch=1,
        grid=(t // tm,),
        in_specs=[
            pl.BlockSpec(memory_space=pl.ANY),
            pl.BlockSpec((tm, TOP_K), lambda i, p: (i, 0)),
            pl.BlockSpec((tm, d // 2), lambda i, p: (i, 0)),
            pl.BlockSpec((tm, d), lambda i, p: (i, 0)),
            pl.BlockSpec((None, SUBLANES, d), lambda i, p: (i // tiles_per_mod, 0, 0)),
            pl.BlockSpec((None, d, sf), lambda i, p: (layer, 0, 0)),
            pl.BlockSpec((None, d, sf), lambda i, p: (layer, 0, 0)),
            pl.BlockSpec((None, sf, d), lambda i, p: (layer, 0, 0)),
            pl.BlockSpec((1, d), lambda i, p: (0, 0)),
            pl.BlockSpec((1, d), lambda i, p: (0, 0)),
        ],
        out_specs=pl.BlockSpec((tm, d), lambda i, p: (i, 0)),
        scratch_shapes=[pltpu.VMEM((2, TOP_K, tm, d // 2), jnp.uint32), pltpu.SemaphoreType.DMA((2,))],
    )
    return pl.pallas_call(
        _moe_combine_kernel,
        out_shape=jax.ShapeDtypeStruct((t, d), F32),
        grid_spec=grid_spec,
        compiler_params=_cparams("arbitrary"),
        name="moe_combine",
    )(dest, yp, gw, hp, x2d, mod, s_gate, s_up, s_down, ln_g.reshape(1, d), ln_b.reshape(1, d))


def _moe_layer(x2d, hp, logits_t, mod, rows_per_mod, layer, router_b, w_gate, w_up, w_down,
               s_gate, s_up, s_down, ln_g, ln_b):
    top_e, gw, rank, counts = _route(logits_t, router_b)
    dest, blk_e, n_used, pad_start, pad_len = _routing_plan(top_e, rank, counts)
    xs = _moe_dispatch(hp, dest, pad_start, pad_len, n_used, blk_e.shape[0] * MOE_TM)
    yp = _moe_ffn(xs, blk_e, n_used, layer, w_gate, w_up, w_down)
    return _moe_combine(yp, dest, gw.T, hp, x2d, mod, layer, s_gate, s_up, s_down, ln_g, ln_b, rows_per_mod)


def _hy_inproj_kernel(x_ref, mod_ref, w_ref, o_ref):
    h = (x_ref[...] * (1.0 + mod_ref[1:2, :]) + mod_ref[0:1, :]).astype(BF16)
    o_ref[...] = _dot(h, w_ref[...])


def _hy_inproj(x2d, mod, w3_bf16, rows_per_mod):
    rows, d = x2d.shape
    n_out = w3_bf16.shape[0]
    tm = HY_TM
    tiles_per_mod = rows_per_mod // tm
    return pl.pallas_call(
        _hy_inproj_kernel,
        out_shape=jax.ShapeDtypeStruct((n_out, rows, d), F32),
        grid=(n_out, rows // tm),
        in_specs=[
            pl.BlockSpec((tm, d), lambda j, i: (i, 0)),
            pl.BlockSpec((None, SUBLANES, d), lambda j, i: (i // tiles_per_mod, 0, 0)),
            pl.BlockSpec((None, d, d), lambda j, i: (j, 0, 0)),
        ],
        out_specs=pl.BlockSpec((None, tm, d), lambda j, i: (j, i, 0)),
        compiler_params=_cparams("arbitrary", "arbitrary"),
        name="hyena_inproj",
    )(x2d, mod, w3_bf16)


def _short_conv_kernel(u_ref, w_ref, b_ref, o_ref):
    u = u_ref[...]
    seq = u.shape[0]
    t = lax.broadcasted_iota(jnp.int32, (seq, 1), 0)
    prev = jnp.where(t == 0, 0.0, pltpu.roll(u, 1, 0))
    nxt = jnp.where(t == seq - 1, 0.0, pltpu.roll(u, seq - 1, 0))
    o_ref[...] = prev * w_ref[0:1, :] + u * w_ref[1:2, :] + nxt * w_ref[2:3, :] + b_ref[...]


def _short_conv(u4, conv_w, conv_b):
    n_out, b, seq, d = u4.shape
    tc = CONV_TC
    return pl.pallas_call(
        _short_conv_kernel,
        out_shape=jax.ShapeDtypeStruct(u4.shape, F32),
        grid=(n_out, b, d // tc),
        in_specs=[
            pl.BlockSpec((None, None, seq, tc), lambda j, i, c: (j, i, 0, c)),
            pl.BlockSpec((None, SUBLANES, tc), lambda j, i, c: (j, 0, c)),
            pl.BlockSpec((None, 1, tc), lambda j, i, c: (j, 0, c)),
        ],
        out_specs=pl.BlockSpec((None, None, seq, tc), lambda j, i, c: (j, i, 0, c)),
        compiler_params=_cparams("arbitrary", "arbitrary", "arbitrary"),
        name="hyena_short_conv",
    )(u4, conv_w, conv_b)


def _filter_kernel(z_ref, t_ref, keep_ref, w1_ref, b1_ref, w2_ref, b2_ref, w3_ref, b3_ref, fq_ref,
                   w4_ref, dl_ref, o_ref):
    fq = fq_ref[...]
    a = jnp.sin(fq * (_dot3(z_ref[...], w1_ref[...]) + b1_ref[...]))
    a = jnp.sin(fq * (_dot3(a, w2_ref[...]) + b2_ref[...]))
    a = jnp.sin(fq * (_dot3(a, w3_ref[...]) + b3_ref[...]))
    hf = _dot3(a, w4_ref[...])
    o_ref[...] = hf * jnp.exp(-t_ref[...] * jnp.abs(dl_ref[...])) * keep_ref[...]


def _hyena_filter(seq, d, f_w1, f_b1, f_w2, f_b2, f_w3, f_b3, f_w4, f_freq):
    n = 2 * seq
    t = jnp.linspace(0.0, 1.0, seq, dtype=F32)[:, None]
    bands = (HY_EMB - 1) // 2
    w = (2.0 * math.pi / seq) * jnp.arange(seq, dtype=F32)
    fb = jnp.linspace(1e-4, bands - 1, bands, dtype=F32)
    fw = w[:, None] * fb[None, :]
    z = jnp.concatenate([t, jnp.cos(fw), -jnp.sin(fw)], axis=-1)
    m = jnp.arange(n)
    src = jnp.where(m < seq, m, jnp.clip(n - m, 0, seq - 1))
    wpad = HY_FILTER_WIDTH - HY_EMB
    z_full = jnp.pad(z[src], ((0, 0), (0, wpad)))
    t_full = t[src]
    keep = (m != seq).astype(F32)[:, None]
    w1p = jnp.pad(f_w1.astype(F32), ((0, wpad), (0, 0)))
    deltas = jnp.linspace(math.log(HY_DECAY_TARGET) / HY_DECAY_FAST,
                          math.log(HY_DECAY_TARGET) / HY_DECAY_SLOW, d, dtype=F32)[None, :]
    fw_ = HY_FILTER_WIDTH
    tiles_per_dir = seq // FILT_TM
    row = lambda v: v.astype(F32).reshape(1, -1)
    small = lambda shape: pl.BlockSpec(shape, lambda i, o: (0, 0))
    return pl.pallas_call(
        _filter_kernel,
        out_shape=jax.ShapeDtypeStruct((HY_ORDER, n, d), F32),
        grid=(n // FILT_TM, HY_ORDER),
        in_specs=[
            pl.BlockSpec((FILT_TM, fw_), lambda i, o: (i, 0)),
            pl.BlockSpec((FILT_TM, 1), lambda i, o: (i, 0)),
            pl.BlockSpec((FILT_TM, 1), lambda i, o: (i, 0)),
            small((fw_, fw_)), small((1, fw_)), small((fw_, fw_)), small((1, fw_)),
            small((fw_, fw_)), small((1, fw_)), small((1, fw_)),
            pl.BlockSpec((fw_, d), lambda i, o: (0, (i // tiles_per_dir) * HY_ORDER + o)),
            small((1, d)),
        ],
        out_specs=pl.BlockSpec((None, FILT_TM, d), lambda i, o: (o, i, 0)),
        compiler_params=_cparams("arbitrary", "arbitrary"),
        name="hyena_filter",
    )(z_full, t_full, keep, w1p, row(f_b1), f_w2.astype(F32), row(f_b2), f_w3.astype(F32), row(f_b3),
      row(f_freq), f_w4.astype(F32), deltas)


def _dft_tables():
    n = FFT_N1 * FFT_N2
    k1 = np.arange(FFT_K1)
    n1 = np.arange(FFT_N1)
    ang1 = 2.0 * np.pi * ((k1[:, None] * n1[None, :]) % FFT_N1) / FFT_N1
    f1 = np.zeros((2 * FFT_K1P, FFT_N1), np.float32)
    f1[:FFT_K1] = np.cos(ang1)
    f1[FFT_K1P:FFT_K1P + FFT_K1] = -np.sin(ang1)
    coef = np.where((k1 == 0) | (k1 == FFT_N1 // 2), 1.0, 2.0) / n
    g = np.zeros((FFT_N1 // 2, 2 * FFT_K1P), np.float32)
    ang1h = ang1[:, :FFT_N1 // 2].T
    g[:, :FFT_K1] = np.cos(ang1h) * coef
    g[:, FFT_K1P:FFT_K1P + FFT_K1] = -np.sin(ang1h) * coef
    n2 = np.arange(FFT_N2)
    k2 = np.arange(FFT_N2)
    mf = np.zeros((FFT_K1P, 2 * FFT_N2, 2 * FFT_N2), np.float32)
    mi = np.zeros((FFT_K1P, 2 * FFT_N2, 2 * FFT_N2), np.float32)
    for a in range(FFT_K1):
        k = a + FFT_N1 * k2
        ang = 2.0 * np.pi * ((k[:, None] * n2[None, :]) % n) / n
        er, ei = np.cos(ang), -np.sin(ang)
        mf[a] = np.block([[er, -ei], [ei, er]])
        mi[a] = np.block([[er.T, ei.T], [-ei.T, er.T]])
    return f1, g, mf, mi


def _dft_stage1_kernel(x_ref, f_ref, re_ref, im_ref):
    n1 = x_ref.shape[0] // FFT_N2
    f = f_ref[...]

    def body(g, carry):
        base = g * FFT_NB
        rhs = jnp.concatenate([x_ref[pl.ds(base + j, n1, stride=FFT_N2), :] for j in range(FFT_NB)], axis=1)
        y = _dot(f, rhs.astype(BF16))
        for j in range(FFT_NB):
            yj = y[:, j * LANES:(j + 1) * LANES]
            re_ref[pl.ds(base + j, FFT_K1P, stride=FFT_N2), :] = yj[:FFT_K1P]
            im_ref[pl.ds(base + j, FFT_K1P, stride=FFT_N2), :] = yj[FFT_K1P:]
        return carry

    lax.fori_loop(0, FFT_N2 // FFT_NB, body, 0, unroll=2)


def _dft_stage1(x4, lead, f1):
    _, b, rows, d = x4.shape
    spec_o = pl.BlockSpec((None, FFT_K1P * FFT_N2, LANES), lambda i, c: (i, 0, c))
    return pl.pallas_call(
        _dft_stage1_kernel,
        out_shape=(jax.ShapeDtypeStruct((b, FFT_K1P * FFT_N2, d), F32),) * 2,
        grid=(b, d // LANES),
        in_specs=[pl.BlockSpec((None, None, rows, LANES), lambda i, c: (lead, i, 0, c)),
                  pl.BlockSpec(f1.shape, lambda i, c: (0, 0))],
        out_specs=(spec_o, spec_o),
        compiler_params=_cparams("arbitrary", "arbitrary"),
        name="dft_stage1",
    )(x4, f1)


def _dft_stage2_kernel(ar_ref, ai_ref, mf_ref, xr_ref, xi_ref):
    a = jnp.concatenate([ar_ref[...], ai_ref[...]], axis=0).astype(BF16)
    x = _dot(mf_ref[...], a)
    xr_ref[...] = x[:FFT_N2]
    xi_ref[...] = x[FFT_N2:]


def _dft_stage2(ar, ai, mf):
    b, _, n2, d = ar.shape
    spec = pl.BlockSpec((None, None, n2, d), lambda k, i: (i, k, 0, 0))
    return pl.pallas_call(
        _dft_stage2_kernel,
        out_shape=(jax.ShapeDtypeStruct(ar.shape, F32),) * 2,
        grid=(FFT_K1P, b),
        in_specs=[spec, spec, pl.BlockSpec((None, 2 * n2, 2 * n2), lambda k, i: (k, 0, 0))],
        out_specs=(spec, spec),
        compiler_params=_cparams("arbitrary", "arbitrary"),
        name="dft_stage2",
    )(ar, ai, mf)


def _spectral_kernel(ar_ref, ai_ref, hr_ref, hi_ref, mf_ref, mi_ref, br_ref, bi_ref):
    a = jnp.concatenate([ar_ref[...], ai_ref[...]], axis=0).astype(BF16)
    x = _dot(mf_ref[...], a)
    xr, xi = x[:FFT_N2], x[FFT_N2:]
    hr, hi = hr_ref[...], hi_ref[...]
    y = jnp.concatenate([xr * hr - xi * hi, xr * hi + xi * hr], axis=0).astype(BF16)
    bv = _dot(mi_ref[...], y)
    br_ref[...] = bv[:FFT_N2]
    bi_ref[...] = bv[FFT_N2:]


def _spectral_multiply(ar, ai, hr, hi, order, mf, mi):
    b, _, n2, d = ar.shape
    spec = pl.BlockSpec((None, None, n2, d), lambda k, i: (i, k, 0, 0))
    spec_h = pl.BlockSpec((None, None, n2, d), lambda k, i: (order, k, 0, 0))
    spec_m = pl.BlockSpec((None, 2 * n2, 2 * n2), lambda k, i: (k, 0, 0))
    return pl.pallas_call(
        _spectral_kernel,
        out_shape=(jax.ShapeDtypeStruct(ar.shape, F32),) * 2,
        grid=(FFT_K1P, b),
        in_specs=[spec, spec, spec_h, spec_h, spec_m, spec_m],
        out_specs=(spec, spec),
        compiler_params=_cparams("arbitrary", "arbitrary"),
        name="dft_spectral",
    )(ar, ai, hr, hi, mf, mi)


def _idft_gate_kernel(br_ref, bi_ref, gr_ref, gi_ref, z_ref, gate_ref, bias_ref, o_ref):
    n1h = z_ref.shape[0] // FFT_N2
    gr, gi = gr_ref[...], gi_ref[...]
    bias = bias_ref[...]

    def body(g, carry):
        base = g * FFT_NB
        rows = lambda ref, j, n: ref[pl.ds(base + j, n, stride=FFT_N2), :]
        br = jnp.concatenate([rows(br_ref, j, FFT_K1P) for j in range(FFT_NB)], axis=1).astype(BF16)
        bi = jnp.concatenate([rows(bi_ref, j, FFT_K1P) for j in range(FFT_NB)], axis=1).astype(BF16)
        y = _dot(gr, br) + _dot(gi, bi)
        for j in range(FFT_NB):
            yj = y[:, j * LANES:(j + 1) * LANES]
            o_ref[pl.ds(base + j, n1h, stride=FFT_N2), :] = (
                rows(gate_ref, j, n1h) * (yj + rows(z_ref, j, n1h) * bias))
        return carry

    lax.fori_loop(0, FFT_N2 // FFT_NB, body, 0, unroll=2)


def _idft_gate(br, bi, gr, gi, z4, z_lead, gate4, gate_lead, bias):
    _, b, seq, d = z4.shape
    spec_b = pl.BlockSpec((None, FFT_K1P * FFT_N2, LANES), lambda i, c: (i, 0, c))
    spec_t = pl.BlockSpec(gr.shape, lambda i, c: (0, 0))
    lead_spec = lambda lead: pl.BlockSpec((None, None, seq, LANES), lambda i, c: (lead, i, 0, c))
    return pl.pallas_call(
        _idft_gate_kernel,
        out_shape=jax.ShapeDtypeStruct((1, b, seq, d), F32),
        grid=(b, d // LANES),
        in_specs=[spec_b, spec_b, spec_t, spec_t, lead_spec(z_lead), lead_spec(gate_lead),
                  pl.BlockSpec((1, LANES), lambda i, c: (0, c))],
        out_specs=lead_spec(0),
        compiler_params=_cparams("arbitrary", "arbitrary"),
        name="idft_gate",
    )(br, bi, gr, gi, z4, gate4, bias)


def _hyena_mixer(x2d, mod, batch, seq, w_in, conv_w, conv_b, f_w1, f_b1, f_w2, f_b2, f_w3, f_b3, f_w4,
                 f_freq, bias_d):
    d = x2d.shape[1]
    assert 2 * seq == FFT_N1 * FFT_N2
    n1h = FFT_N1 // 2
    w3 = w_in.reshape(d, 3, d).transpose(1, 0, 2).astype(BF16)
    u = _hy_inproj(x2d, mod, w3, seq)
    cw = jnp.pad(conv_w.reshape(3, 3, d).transpose(1, 0, 2), ((0, 0), (0, SUBLANES - 3), (0, 0)))
    cb = conv_b.reshape(3, 1, d)
    u4 = _short_conv(u.reshape(3, batch, seq, d), cw, cb)
    f1, g, mf, mi = _dft_tables()
    f1 = jnp.asarray(f1).astype(BF16)
    g = jnp.asarray(g).astype(BF16)
    gr, gi = g[:, :FFT_K1P], g[:, FFT_K1P:]
    mf = jnp.asarray(mf).astype(BF16)
    mi = jnp.asarray(mi).astype(BF16)
    rows4 = lambda a: a.reshape(a.shape[0], FFT_K1P, FFT_N2, d)
    rows3 = lambda a: a.reshape(a.shape[0], FFT_K1P * FFT_N2, d)
    kern = _hyena_filter(seq, d, f_w1, f_b1, f_w2, f_b2, f_w3, f_b3, f_w4, f_freq)
    kr, ki = _dft_stage1(kern[None], 0, f1)
    hr, hi = _dft_stage2(rows4(kr), rows4(ki), mf)
    z4, z_lead = u4, 0
    f1h = f1[:, :n1h]
    for o in range(HY_ORDER):
        ar, ai = _dft_stage1(z4, z_lead, f1h)
        br, bi = _spectral_multiply(rows4(ar), rows4(ai), hr, hi, o, mf, mi)
        z4 = _idft_gate(rows3(br), rows3(bi), gr, gi, z4, z_lead, u4, 1 + o,
                        bias_d[o].astype(F32).reshape(1, d))
        z_lead = 0
    return z4.reshape(batch * seq, d)


def _mod_table(m_layer, d):
    m6 = m_layer.reshape(m_layer.shape[0], 6, d)
    return jnp.pad(m6, ((0, 0), (0, SUBLANES - 6), (0, 0)))


def kernel(x, c, ctx, c_ctx, ada_w, ada_b, ln_g, ln_b, attn_w_in, attn_w_out, attn_sink, attn_q_gain, attn_k_gain, hy_w_in, hy_conv_w, hy_conv_b, hy_f_w1, hy_f_b1, hy_f_w2, hy_f_b2, hy_f_w3, hy_f_b3, hy_f_w4, hy_f_freq, hy_bias_d, hy_w_out, moe_router_w, moe_router_b, moe_w_gate, moe_w_up, moe_w_down, moe_s_gate, moe_s_up, moe_s_down):
    batch, seq, d = x.shape
    n_ctx = ctx.shape[1]
    depth = ada_w.shape[0]
    assert depth == DEPTH and HEAD_DIM == LANES and batch + 1 <= SUBLANES
    cc = jnp.concatenate([c, c_ctx[None, :], jnp.zeros((SUBLANES - batch - 1, d), c.dtype)], axis=0)
    m_all = _ada_modulation(cc.astype(F32), ada_w, ada_b)
    x2d = x.reshape(batch * seq, d)
    cos_t, sin_t = _rope_tables(seq)
    s_gate, s_up, s_down = moe_s_gate.astype(BF16), moe_s_up.astype(BF16), moe_s_down.astype(BF16)
    for l in range(depth):
        mod = _mod_table(m_all[l], d)
        i = l // 2
        router_wt = moe_router_w[l].astype(F32).T
        if l % 2 == 0:
            w_in = attn_w_in[i].astype(BF16)
            qkv = _attn_inproj(x2d, mod, w_in, cos_t, sin_t, attn_q_gain[i], attn_k_gain[i],
                               seq, 0, INPROJ_TM)
            ones = jnp.ones((n_ctx, HEAD_DIM), F32)
            qkv_ctx = _attn_inproj(ctx.reshape(batch * n_ctx, d), mod, w_in, ones, jnp.zeros_like(ones),
                                   attn_q_gain[i], attn_k_gain[i], batch * n_ctx, batch, n_ctx)
            qkv = qkv.reshape(batch, seq, -1)
            qkv_ctx = qkv_ctx.reshape(batch, n_ctx, -1)
            o_a = _win_attention(qkv, qkv_ctx, attn_sink[i].astype(F32))
            o_b = _dense_attention(qkv, qkv_ctx)
            w_out = attn_w_out[i].astype(BF16)
            na = A_Q_HEADS * HEAD_DIM
            acts = [o_a.reshape(batch * seq, na), o_b.reshape(batch * seq, -1)]
            weights = [w_out[:na], w_out[na:]]
        else:
            z = _hyena_mixer(x2d, mod, batch, seq, hy_w_in[i], hy_conv_w[i], hy_conv_b[i], hy_f_w1[i],
                             hy_f_b1[i], hy_f_w2[i], hy_f_b2[i], hy_f_w3[i], hy_f_b3[i], hy_f_w4[i],
                             hy_f_freq[i], hy_bias_d[i])
            acts = [z]
            weights = [hy_w_out[i].astype(BF16)]
        x2d, hp, logits_t = _outproj_ln(acts, weights, x2d, mod, ln_g[l, 0], ln_b[l, 0], router_wt, seq)
        x2d = _moe_layer(x2d, hp, logits_t, mod, seq, l, moe_router_b[l], moe_w_gate, moe_w_up, moe_w_down,
                         s_gate, s_up, s_down, ln_g[l, 1], ln_b[l, 1])
    return x2d.reshape(batch, seq, d)
```

```python
import functools
import math

import numpy as np
import jax
import jax.numpy as jnp
from jax import lax
from jax.experimental import pallas as pl
from jax.experimental.pallas import tpu as pltpu

F32 = jnp.float32
BF16 = jnp.bfloat16

DEPTH = 2
GRID_W = 64
HEAD_DIM = 128
A_Q_HEADS, A_KV_HEADS, B_Q_HEADS, B_KV_HEADS = 8, 2, 8, 2
GQA_GROUP = A_Q_HEADS // A_KV_HEADS
WINDOW = 128
Q_BLOCK = 128
ROPE_THETA = 10000.0
QK_NORM_EPS = 1e-6
LN_EPS = 1e-5
HY_ORDER = 2
HY_EMB = 33
HY_FILTER_WIDTH = 64
HY_DECAY_FAST, HY_DECAY_SLOW, HY_DECAY_TARGET = 0.3, 1.5, 1e-2
N_EXPERTS = 64
TOP_K = 8
N_GROUPS = 8
TOP_GROUPS = 4
ROUTED_SCALE = 2.5
DEEPNORM_ALPHA = (2 * DEPTH) ** 0.25

LANES = 128
SUBLANES = 8
VMEM_LIMIT = 56 * 1024 * 1024

ADA_TN = 1024
INPROJ_TM = 512
DENSE_TQ = 256
DENSE_TK = 1024
OUTPROJ_TM = 512
OUTPROJ_CHUNK = 128
HY_TM = 512
CONV_TC = 512
MOE_TM = 512
MOE_CHAINS = 1
ROUTE_TT = 512
DISP_TM = 256
COMB_TM = 256
FFT_N1, FFT_N2 = 64, 128
FFT_K1 = FFT_N1 // 2 + 1
FFT_K1P = 40
FFT_NB = 8
FILT_TM = 512


def _cparams(*sem):
    return pltpu.CompilerParams(dimension_semantics=sem, vmem_limit_bytes=VMEM_LIMIT)


def _silu(v):
    return v * jax.nn.sigmoid(v)


def _split_bf16(v):
    hi = v.astype(BF16)
    lo = (v - hi.astype(F32)).astype(BF16)
    return hi, lo


def _dot(a, b):
    return jnp.dot(a, b, preferred_element_type=F32)


def _dot_nt(a, b):
    return lax.dot_general(a, b, (((1,), (1,)), ((), ())), preferred_element_type=F32)


def _pack_bf16_pairs(v):
    n = v.shape[1] // 2
    bits = pltpu.bitcast(v.astype(BF16).astype(F32), jnp.uint32)
    return (bits[:, :n] >> 16) | (bits[:, n:] & jnp.uint32(0xFFFF0000))


def _unpack_bf16_pairs(p):
    lo = pltpu.bitcast(p << 16, F32)
    hi = pltpu.bitcast(p & jnp.uint32(0xFFFF0000), F32)
    return lo, hi


def _dot3(a, b):
    ah, al = _split_bf16(a)
    bh, bl = _split_bf16(b)
    return _dot(ah, bh) + _dot(al, bh) + _dot(ah, bl)


def _ada_kernel(c_ref, w_ref, b_ref, o_ref):
    s = _silu(c_ref[...]).astype(BF16)
    o_ref[...] = _dot(s, w_ref[...].astype(BF16)) + b_ref[...]


def _ada_modulation(cc, ada_w, ada_b):
    depth, d, n = ada_w.shape
    rows = cc.shape[0]
    return pl.pallas_call(
        _ada_kernel,
        out_shape=jax.ShapeDtypeStruct((depth, rows, n), F32),
        grid=(depth, n // ADA_TN),
        in_specs=[
            pl.BlockSpec((rows, d), lambda l, j: (0, 0)),
            pl.BlockSpec((None, d, ADA_TN), lambda l, j: (l, 0, j)),
            pl.BlockSpec((None, 1, ADA_TN), lambda l, j: (l, 0, j)),
        ],
        out_specs=pl.BlockSpec((None, rows, ADA_TN), lambda l, j: (l, 0, j)),
        compiler_params=_cparams("arbitrary", "arbitrary"),
        name="ada_modulation",
    )(cc, ada_w, ada_b.reshape(depth, 1, n))


_N_SLOTS = (A_Q_HEADS + 2 * A_KV_HEADS + B_Q_HEADS + 2 * B_KV_HEADS)
_QA0, _KA0, _VA0 = 0, A_Q_HEADS, A_Q_HEADS + A_KV_HEADS
_QB0 = A_Q_HEADS + 2 * A_KV_HEADS
_KB0, _VB0 = _QB0 + B_Q_HEADS, _QB0 + B_Q_HEADS + B_KV_HEADS


def _slot_kind(j):
    if j < _KA0:
        return "qa"
    if j < _VA0:
        return "ka"
    if j < _QB0:
        return "v"
    if j < _KB0:
        return "qb"
    if j < _VB0:
        return "kb"
    return "v"


def _attn_inproj_kernel(x_ref, mod_ref, w_ref, cos_ref, sin_ref, qg_ref, kg_ref, o_ref):
    x = x_ref[...]
    h = (x * (1.0 + mod_ref[1:2, :]) + mod_ref[0:1, :]).astype(BF16)
    y = _dot(h, w_ref[...])
    cos = cos_ref[...]
    sin = sin_ref[...]
    lane = lax.broadcasted_iota(jnp.int32, cos.shape, 1)
    low_half = (lane & (HEAD_DIM // 2 - 1)) < (HEAD_DIM // 4)
    q_scale = HEAD_DIM ** -0.5
    for j in range(_N_SLOTS):
        kind = _slot_kind(j)
        blk = y[:, j * HEAD_DIM:(j + 1) * HEAD_DIM]
        if kind in ("qb", "kb"):
            gain = qg_ref[...] if kind == "qb" else kg_ref[...]
            ms = jnp.mean(blk * blk, axis=-1, keepdims=True)
            blk = blk * lax.rsqrt(ms + QK_NORM_EPS) * gain
        if kind != "v":
            partner = jnp.where(low_half,
                                pltpu.roll(blk, HEAD_DIM - HEAD_DIM // 4, 1),
                                pltpu.roll(blk, HEAD_DIM // 4, 1))
            blk = blk * cos + partner * sin
        if kind in ("qa", "qb"):
            blk = blk * q_scale
        o_ref[:, j * HEAD_DIM:(j + 1) * HEAD_DIM] = blk.astype(o_ref.dtype)


def _attn_inproj(x2d, mod, w_bf16, cos_t, sin_t, q_gain, k_gain, rows_per_mod, mod_base, tm):
    rows, d = x2d.shape
    n = w_bf16.shape[1]
    tiles_per_mod = rows_per_mod // tm
    tab_tiles = cos_t.shape[0] // tm
    return pl.pallas_call(
        _attn_inproj_kernel,
        out_shape=jax.ShapeDtypeStruct((rows, n), BF16),
        grid=(rows // tm,),
        in_specs=[
            pl.BlockSpec((tm, d), lambda i: (i, 0)),
            pl.BlockSpec((None, SUBLANES, d), lambda i: (mod_base + i // tiles_per_mod, 0, 0)),
            pl.BlockSpec((d, n), lambda i: (0, 0), pipeline_mode=pl.Buffered(1)),
            pl.BlockSpec((tm, HEAD_DIM), lambda i: (i % tab_tiles, 0)),
            pl.BlockSpec((tm, HEAD_DIM), lambda i: (i % tab_tiles, 0)),
            pl.BlockSpec((1, HEAD_DIM), lambda i: (0, 0)),
            pl.BlockSpec((1, HEAD_DIM), lambda i: (0, 0)),
        ],
        out_specs=pl.BlockSpec((tm, n), lambda i: (i, 0)),
        compiler_params=_cparams("arbitrary"),
        name="attn_inproj",
    )(x2d, mod, w_bf16, cos_t, sin_t, q_gain.reshape(1, -1), k_gain.reshape(1, -1))


def _rope_tables(seq):
    rows = seq // GRID_W
    row = jnp.repeat(jnp.arange(rows, dtype=F32), GRID_W)
    col = jnp.tile(jnp.arange(GRID_W, dtype=F32), rows)
    n_freq = HEAD_DIM // 4
    inv = ROPE_THETA ** (-jnp.arange(n_freq, dtype=F32) / n_freq)
    ang_r, ang_c = row[:, None] * inv, col[:, None] * inv
    cos = jnp.concatenate([jnp.cos(ang_r)] * 2 + [jnp.cos(ang_c)] * 2, axis=-1)
    sin = jnp.concatenate([-jnp.sin(ang_r), jnp.sin(ang_r), -jnp.sin(ang_c), jnp.sin(ang_c)], axis=-1)
    return cos, sin


def _win_attn_kernel(sink_ref, q_ref, k_ref, v_ref, kc_ref, vc_ref, o_ref):
    seq = q_ref.shape[0]
    hk = pl.program_id(1)
    n_loc = 3 * Q_BLOCK
    rows = GQA_GROUP * Q_BLOCK
    row = lax.broadcasted_iota(jnp.int32, (rows, 1), 0)
    col = lax.broadcasted_iota(jnp.int32, (1, n_loc), 1)
    sink_col = jnp.full((rows, 1), sink_ref[hk * GQA_GROUP + GQA_GROUP - 1], F32)
    for g in range(GQA_GROUP - 2, -1, -1):
        sink_col = jnp.where(row < (g + 1) * Q_BLOCK, sink_ref[hk * GQA_GROUP + g], sink_col)
    kc = kc_ref[...]
    vc = vc_ref[...]

    def body(n, carry):
        q0 = pl.multiple_of(n * Q_BLOCK, Q_BLOCK)
        start = pl.multiple_of(jnp.clip(q0 - Q_BLOCK, 0, seq - n_loc), Q_BLOCK)
        q = jnp.concatenate(
            [q_ref[pl.ds(q0, Q_BLOCK), g * HEAD_DIM:(g + 1) * HEAD_DIM] for g in range(GQA_GROUP)], axis=0)
        kw = k_ref[pl.ds(start, n_loc), :]
        vw = v_ref[pl.ds(start, n_loc), :]
        s_loc = _dot_nt(q, kw)
        s_ctx = _dot_nt(q, kc)
        qpos = q0 + (row & (Q_BLOCK - 1))
        kpos = start + col
        s_loc = jnp.where(jnp.abs(qpos - kpos) <= WINDOW, s_loc, -jnp.inf)
        m = jnp.maximum(jnp.maximum(jnp.max(s_loc, -1, keepdims=True), jnp.max(s_ctx, -1, keepdims=True)),
                        sink_col)
        p_loc = jnp.exp(s_loc - m)
        p_ctx = jnp.exp(s_ctx - m)
        denom = (jnp.sum(p_loc, -1, keepdims=True) + jnp.sum(p_ctx, -1, keepdims=True)
                 + jnp.exp(sink_col - m))
        o = (_dot(p_loc.astype(BF16), vw) + _dot(p_ctx.astype(BF16), vc)) / denom
        for g in range(GQA_GROUP):
            o_ref[pl.ds(q0, Q_BLOCK), g * HEAD_DIM:(g + 1) * HEAD_DIM] = (
                o[g * Q_BLOCK:(g + 1) * Q_BLOCK].astype(o_ref.dtype))
        return carry

    lax.fori_loop(0, seq // Q_BLOCK, body, 0)


def _win_attention(qkv, qkv_ctx, sink):
    b, s, _ = qkv.shape
    c = qkv_ctx.shape[1]
    gw = GQA_GROUP * HEAD_DIM
    return pl.pallas_call(
        _win_attn_kernel,
        out_shape=jax.ShapeDtypeStruct((b, s, A_Q_HEADS * HEAD_DIM), BF16),
        grid=(b, A_KV_HEADS),
        in_specs=[
            pl.BlockSpec(memory_space=pltpu.SMEM),
            pl.BlockSpec((None, s, gw), lambda i, h: (i, 0, h)),
            pl.BlockSpec((None, s, HEAD_DIM), lambda i, h: (i, 0, _KA0 + h)),
            pl.BlockSpec((None, s, HEAD_DIM), lambda i, h: (i, 0, _VA0 + h)),
            pl.BlockSpec((None, c, HEAD_DIM), lambda i, h: (i, 0, _KA0 + h)),
            pl.BlockSpec((None, c, HEAD_DIM), lambda i, h: (i, 0, _VA0 + h)),
        ],
        out_specs=pl.BlockSpec((None, s, gw), lambda i, h: (i, 0, h)),
        compiler_params=_cparams("arbitrary", "arbitrary"),
        name="win_attention",
    )(sink, qkv, qkv, qkv, qkv_ctx, qkv_ctx)


def _dense_attn_kernel(q_ref, k_ref, v_ref, kc_ref, vc_ref, o_ref, m_sc, l_sc, acc_sc):
    seq = k_ref.shape[0]
    m_sc[...] = jnp.full(m_sc.shape, -jnp.inf, F32)
    l_sc[...] = jnp.zeros(l_sc.shape, F32)
    acc_sc[...] = jnp.zeros(acc_sc.shape, F32)

    def step(kb, vb):
        reps = kb.shape[0] // LANES
        for g in range(GQA_GROUP):
            s = _dot_nt(q_ref[:, g * HEAD_DIM:(g + 1) * HEAD_DIM], kb)
            m_prev = m_sc[g]
            m_new = jnp.maximum(m_prev, jnp.max(s, -1, keepdims=True))
            alpha = jnp.exp(m_prev - m_new)
            p = jnp.exp(s - jnp.concatenate([m_new] * reps, axis=1))
            l_sc[g] = alpha * l_sc[g] + jnp.sum(p, -1, keepdims=True)
            acc_sc[g] = alpha * acc_sc[g] + _dot(p.astype(BF16), vb)
            m_sc[g] = m_new

    def body(j, carry):
        k0 = pl.multiple_of(j * DENSE_TK, DENSE_TK)
        step(k_ref[pl.ds(k0, DENSE_TK), :], v_ref[pl.ds(k0, DENSE_TK), :])
        return carry

    lax.fori_loop(0, seq // DENSE_TK, body, 0)
    step(kc_ref[...], vc_ref[...])
    for g in range(GQA_GROUP):
        o_ref[:, g * HEAD_DIM:(g + 1) * HEAD_DIM] = (acc_sc[g] / l_sc[g]).astype(o_ref.dtype)


def _dense_attention(qkv, qkv_ctx):
    b, s, _ = qkv.shape
    c = qkv_ctx.shape[1]
    gw = GQA_GROUP * HEAD_DIM
    qb_blk = _QB0 * HEAD_DIM // gw
    assert s % DENSE_TK == 0 and s % DENSE_TQ == 0
    stat = pltpu.VMEM((GQA_GROUP, DENSE_TQ, LANES), F32)
    return pl.pallas_call(
        _dense_attn_kernel,
        out_shape=jax.ShapeDtypeStruct((b, s, B_Q_HEADS * HEAD_DIM), BF16),
        grid=(b, B_KV_HEADS, s // DENSE_TQ),
        in_specs=[
            pl.BlockSpec((None, DENSE_TQ, gw), lambda i, h, t: (i, t, qb_blk + h)),
            pl.BlockSpec((None, s, HEAD_DIM), lambda i, h, t: (i, 0, _KB0 + h)),
            pl.BlockSpec((None, s, HEAD_DIM), lambda i, h, t: (i, 0, _VB0 + h)),
            pl.BlockSpec((None, c, HEAD_DIM), lambda i, h, t: (i, 0, _KB0 + h)),
            pl.BlockSpec((None, c, HEAD_DIM), lambda i, h, t: (i, 0, _VB0 + h)),
        ],
        out_specs=pl.BlockSpec((None, DENSE_TQ, gw), lambda i, h, t: (i, t, h)),
        scratch_shapes=[stat, stat, stat],
        compiler_params=_cparams("arbitrary", "arbitrary", "arbitrary"),
        name="dense_attention",
    )(qkv, qkv, qkv, qkv_ctx, qkv_ctx)


def _layer_norm(r, g, b):
    mu = jnp.mean(r, axis=-1, keepdims=True)
    c = r - mu
    var = jnp.mean(c * c, axis=-1, keepdims=True)
    return c * lax.rsqrt(var + LN_EPS) * g + b


def _outproj_ln_kernel(n_in, *refs):
    a_refs = refs[:n_in]
    w_refs = refs[n_in:2 * n_in]
    x_ref, mod_ref, lng_ref, lnb_ref, rwh_ref, rwl_ref, xo_ref, h2_ref, lg_ref = refs[2 * n_in:]
    y = _dot(a_refs[0][...].astype(BF16), w_refs[0][...])
    for a_ref, w_ref in zip(a_refs[1:], w_refs[1:]):
        y = y + _dot(a_ref[...].astype(BF16), w_ref[...])
    rwh, rwl = rwh_ref[...], rwl_ref[...]
    for c in range(y.shape[0] // OUTPROJ_CHUNK):
        sl = slice(c * OUTPROJ_CHUNK, (c + 1) * OUTPROJ_CHUNK)
        r = DEEPNORM_ALPHA * x_ref[sl, :] + mod_ref[2:3, :] * y[sl]
        xn = _layer_norm(r, lng_ref[...], lnb_ref[...])
        xo_ref[sl, :] = xn
        h2 = xn * (1.0 + mod_ref[4:5, :]) + mod_ref[3:4, :]
        h2_ref[sl, :] = _pack_bf16_pairs(h2)
        hi, lo = _split_bf16(h2)
        lg_ref[:, sl] = _dot_nt(rwh, hi) + _dot_nt(rwh, lo) + _dot_nt(rwl, hi)


def _outproj_ln(acts, weights, x2d, mod, ln_g, ln_b, router_wt, rows_per_mod):
    rows, d = x2d.shape
    tm = OUTPROJ_TM
    n_in = len(acts)
    n_e = router_wt.shape[0]
    tiles_per_mod = rows_per_mod // tm
    rwh, rwl = _split_bf16(router_wt)
    in_specs = (
        [pl.BlockSpec((tm, a.shape[1]), lambda i: (i, 0)) for a in acts]
        + [pl.BlockSpec(w.shape, lambda i: (0, 0), pipeline_mode=pl.Buffered(1)) for w in weights]
        + [
            pl.BlockSpec((tm, d), lambda i: (i, 0)),
            pl.BlockSpec((None, SUBLANES, d), lambda i: (i // tiles_per_mod, 0, 0)),
            pl.BlockSpec((1, d), lambda i: (0, 0)),
            pl.BlockSpec((1, d), lambda i: (0, 0)),
            pl.BlockSpec((n_e, d), lambda i: (0, 0)),
            pl.BlockSpec((n_e, d), lambda i: (0, 0)),
        ])
    return pl.pallas_call(
        functools.partial(_outproj_ln_kernel, n_in),
        out_shape=(jax.ShapeDtypeStruct((rows, d), F32), jax.ShapeDtypeStruct((rows, d // 2), jnp.uint32),
                   jax.ShapeDtypeStruct((n_e, rows), F32)),
        grid=(rows // tm,),
        in_specs=in_specs,
        out_specs=(pl.BlockSpec((tm, d), lambda i: (i, 0)), pl.BlockSpec((tm, d // 2), lambda i: (i, 0)),
                   pl.BlockSpec((n_e, tm), lambda i: (0, i))),
        compiler_params=_cparams("arbitrary"),
        name="outproj_ln",
    )(*acts, *weights, x2d, mod, ln_g.reshape(1, d), ln_b.reshape(1, d), rwh, rwl)


def _route_kernel(lg_ref, b_ref, tri_ref, e_ref, w_ref, r_ref, cnt_ref, carry_sc):
    @pl.when(pl.program_id(0) == 0)
    def _():
        carry_sc[...] = jnp.zeros(carry_sc.shape, F32)

    tt = lg_ref.shape[1]
    gsz = N_EXPERTS // N_GROUPS
    s = jax.nn.sigmoid(lg_ref[...])
    biased = s + b_ref[...]
    sub = lax.broadcasted_iota(jnp.int32, (gsz, tt), 0).astype(F32)
    rows = []
    for g in range(N_GROUPS):
        v = biased[g * gsz:(g + 1) * gsz]
        m1 = jnp.max(v, axis=0, keepdims=True)
        first = jnp.min(jnp.where(v == m1, sub, float(gsz)), axis=0, keepdims=True)
        m2 = jnp.max(jnp.where(sub == first, -jnp.inf, v), axis=0, keepdims=True)
        rows.append(m1 + m2)
    gs = jnp.concatenate(rows, axis=0)
    gidx = lax.broadcasted_iota(jnp.int32, (N_GROUPS, tt), 0)
    grank = jnp.zeros((N_GROUPS, tt), F32)
    for g2 in range(N_GROUPS):
        row = gs[g2:g2 + 1]
        ahead = jnp.where(row > gs, 1.0, jnp.where(row == gs, jnp.where(gidx > g2, 1.0, 0.0), 0.0))
        grank = grank + ahead
    gkeep = jnp.where(grank < TOP_GROUPS, 1.0, 0.0)
    keep = jnp.concatenate([jnp.broadcast_to(gkeep[g:g + 1], (gsz, tt)) for g in range(N_GROUPS)], axis=0)
    cur = jnp.where(keep > 0.5, biased, -jnp.inf)
    eidx = lax.broadcasted_iota(jnp.int32, (N_EXPERTS, tt), 0).astype(F32)
    chosen = jnp.zeros((N_EXPERTS, tt), F32)
    e_rows, s_rows = [], []
    for _ in range(TOP_K):
        m = jnp.max(cur, axis=0, keepdims=True)
        idx = jnp.min(jnp.where(cur == m, eidx, float(N_EXPERTS)), axis=0, keepdims=True)
        hit = eidx == idx
        s_rows.append(jnp.sum(jnp.where(hit, s, 0.0), axis=0, keepdims=True))
        e_rows.append(idx)
        chosen = jnp.where(hit, 1.0, chosen)
        cur = jnp.where(hit, -jnp.inf, cur)
    top_e = jnp.concatenate(e_rows, axis=0)
    sc = jnp.concatenate(s_rows, axis=0)
    w_ref[...] = sc / jnp.sum(sc, axis=0, keepdims=True) * ROUTED_SCALE
    e_ref[...] = top_e.astype(jnp.int32)
    carry = carry_sc[...]
    cnt = _dot(chosen.astype(BF16), tri_ref[...]) + jnp.concatenate([carry] * (tt // LANES), axis=1)
    rank = jnp.zeros((TOP_K, tt), F32)
    for e in range(N_EXPERTS):
        rank = rank + jnp.where(top_e == float(e), cnt[e:e + 1], 0.0)
    r_ref[...] = rank.astype(jnp.int32)
    carry = carry + jnp.sum(chosen, axis=1, keepdims=True)
    carry_sc[...] = carry
    cnt_ref[...] = carry


def _route(logits_t, router_b):
    n_e, t = logits_t.shape
    tt = ROUTE_TT
    tri = (jnp.arange(tt)[:, None] < jnp.arange(tt)[None, :]).astype(BF16)
    tok_spec = pl.BlockSpec((TOP_K, tt), lambda i: (0, i))
    return pl.pallas_call(
        _route_kernel,
        out_shape=(jax.ShapeDtypeStruct((TOP_K, t), jnp.int32), jax.ShapeDtypeStruct((TOP_K, t), F32),
                   jax.ShapeDtypeStruct((TOP_K, t), jnp.int32), jax.ShapeDtypeStruct((n_e, LANES), F32)),
        grid=(t // tt,),
        in_specs=[pl.BlockSpec((n_e, tt), lambda i: (0, i)),
                  pl.BlockSpec((n_e, 1), lambda i: (0, 0)),
                  pl.BlockSpec((tt, tt), lambda i: (0, 0))],
        out_specs=(tok_spec, tok_spec, tok_spec, pl.BlockSpec((n_e, LANES), lambda i: (0, 0))),
        scratch_shapes=[pltpu.VMEM((n_e, LANES), F32)],
        compiler_params=_cparams("arbitrary"),
        name="moe_route",
    )(logits_t, router_b.astype(F32).reshape(n_e, 1), tri)


def _routing_plan(top_e, rank, counts_f):
    t = top_e.shape[1]
    counts = counts_f[:, 0].astype(jnp.int32)
    pcounts = (counts + MOE_TM - 1) // MOE_TM * MOE_TM
    pends = jnp.cumsum(pcounts)
    pstarts = pends - pcounts
    first_row = jnp.sum(jnp.where(top_e[..., None] == jnp.arange(N_EXPERTS, dtype=jnp.int32), pstarts, 0), axis=-1)
    dest = (first_row + rank).reshape(-1)
    n_blocks = t * TOP_K // MOE_TM + N_EXPERTS
    n_used = (pends[-1] // MOE_TM).astype(jnp.int32)
    blk_start = jnp.minimum(jnp.arange(n_blocks, dtype=jnp.int32), n_used - 1) * MOE_TM
    blk_e = jnp.sum((pends[None, :] <= blk_start[:, None]).astype(jnp.int32), axis=1)
    blk_e = jnp.minimum(blk_e, N_EXPERTS - 1).astype(jnp.int32)
    return dest, blk_e, n_used.reshape(1), (pstarts + counts).astype(jnp.int32), (pcounts - counts).astype(jnp.int32)


_PAD_PIECES = tuple(1 << b for b in range(MOE_TM.bit_length() - 2, SUBLANES.bit_length() - 2, -1))


def _dispatch_kernel(dest_ref, pad_start, pad_len, n_used, hp_ref, xs_hbm, zbuf, sem, zsem):
    i = pl.program_id(0)
    tm = hp_ref.shape[0]
    t_total = pl.num_programs(0) * tm
    n_blocks = xs_hbm.shape[0] // MOE_TM

    def zero_copy(row0, n):
        return pltpu.make_async_copy(zbuf.at[pl.ds(0, n), :], xs_hbm.at[pl.ds(row0, n), :], zsem)

    def for_each_pad_row(fn):
        def per_expert(e, carry):
            start, length = pad_start[e], pad_len[e]
            head = jnp.minimum((-start) & (SUBLANES - 1), length)

            def per_row(j, c):
                fn(zero_copy(start + j, 1))
                return c
            lax.fori_loop(0, head, per_row, 0)
            rest = length - head
            for piece in _PAD_PIECES:
                @pl.when((rest & piece) != 0)
                def _():
                    fn(zero_copy(pl.multiple_of(start + head + (rest & ~(2 * piece - 1)), SUBLANES), piece))
            return carry
        lax.fori_loop(0, N_EXPERTS, per_expert, 0)

        def per_block(blk, carry):
            fn(zero_copy(pl.multiple_of(blk * MOE_TM, MOE_TM), MOE_TM))
            return carry
        lax.fori_loop(n_used[0], n_blocks, per_block, 0)

    @pl.when(i == 0)
    def _():
        zbuf[...] = jnp.zeros(zbuf.shape, zbuf.dtype)
        for_each_pad_row(lambda cp: cp.start())

    def issue(r, carry):
        src = hp_ref.at[pl.ds(r, 1), :]
        for k in range(TOP_K):
            d = dest_ref[k * t_total + i * tm + r]
            pltpu.make_async_copy(src, xs_hbm.at[pl.ds(d, 1), :], sem).start(priority=k % 2)
        return carry

    lax.fori_loop(0, tm, issue, 0)
    for k in range(TOP_K):
        pltpu.make_async_copy(hp_ref, xs_hbm.at[pl.ds(0, tm), :], sem).wait()

    @pl.when(i == 0)
    def _():
        for_each_pad_row(lambda cp: cp.wait())


def _moe_dispatch(hp, dest, pad_start, pad_len, n_used, n_rows):
    t, dh = hp.shape
    tm = DISP_TM
    grid_spec = pltpu.PrefetchScalarGridSpec(
        num_scalar_prefetch=4,
        grid=(t // tm,),
        in_specs=[pl.BlockSpec((tm, dh), lambda i, ds_, ps, pn, nu: (i, 0))],
        out_specs=pl.BlockSpec(memory_space=pl.ANY),
        scratch_shapes=[pltpu.VMEM((MOE_TM, dh), jnp.uint32), pltpu.SemaphoreType.DMA,
                        pltpu.SemaphoreType.DMA],
    )
    return pl.pallas_call(
        _dispatch_kernel,
        out_shape=jax.ShapeDtypeStruct((n_rows, dh), jnp.uint32),
        grid_spec=grid_spec,
        compiler_params=_cparams("arbitrary"),
        name="moe_dispatch",
    )(dest, pad_start, pad_len, n_used, hp)


def _moe_ffn_kernel(blk_e, n_used, xs_ref, wg_ref, wu_ref, wd_ref, y_ref, wgb, wub, wdb):
    i = pl.program_id(0)

    @pl.when(i < n_used[0])
    def _():
        @pl.when((i == 0) | (blk_e[i] != blk_e[jnp.maximum(i - 1, 0)]))
        def _():
            wgb[...] = wg_ref[...].astype(BF16)
            wub[...] = wu_ref[...].astype(BF16)
            wdb[...] = wd_ref[...].astype(BF16)

        rows = xs_ref.shape[0] // MOE_CHAINS
        for c in range(MOE_CHAINS):
            sl = pl.ds(c * rows, rows)
            lo, hi = _unpack_bf16_pairs(xs_ref[sl, :])
            x = jnp.concatenate([lo, hi], axis=1).astype(BF16)
            hid = (_silu(_dot(x, wgb[...])) * _dot(x, wub[...])).astype(BF16)
            y_ref[sl, :] = _pack_bf16_pairs(_dot(hid, wdb[...]))

    @pl.when(i >= n_used[0])
    def _():
        y_ref[...] = jnp.zeros(y_ref.shape, y_ref.dtype)


def _moe_ffn(xs, blk_e, n_used, layer, w_gate, w_up, w_down):
    n_rows, dh = xs.shape
    d = 2 * dh
    n_blocks = blk_e.shape[0]
    ff = w_gate.shape[3]
    tm = MOE_TM
    row_blk = lambda i, be, nu: (jnp.minimum(i, nu[0] - 1), 0)
    grid_spec = pltpu.PrefetchScalarGridSpec(
        num_scalar_prefetch=2,
        grid=(n_blocks,),
        in_specs=[
            pl.BlockSpec((tm, dh), row_blk),
            pl.BlockSpec((None, None, d, ff), lambda i, be, nu: (layer, be[i], 0, 0)),
            pl.BlockSpec((None, None, d, ff), lambda i, be, nu: (layer, be[i], 0, 0)),
            pl.BlockSpec((None, None, ff, d), lambda i, be, nu: (layer, be[i], 0, 0)),
        ],
        out_specs=pl.BlockSpec((tm, dh), lambda i, be, nu: (i, 0)),
        scratch_shapes=[pltpu.VMEM((d, ff), BF16), pltpu.VMEM((d, ff), BF16), pltpu.VMEM((ff, d), BF16)],
    )
    return pl.pallas_call(
        _moe_ffn_kernel,
        out_shape=jax.ShapeDtypeStruct((n_rows, dh), jnp.uint32),
        grid_spec=grid_spec,
        compiler_params=_cparams("arbitrary"),
        name="moe_ffn",
    )(blk_e, n_used, xs, w_gate, w_up, w_down)


def _moe_combine_kernel(dest_ref, yp_hbm, gw_ref, hp_ref, x_ref, mod_ref, sg_ref, su_ref, sd_ref,
                        lng_ref, lnb_ref, o_ref, gbuf, sem):
    i = pl.program_id(0)
    n_tiles = pl.num_programs(0)
    tm = hp_ref.shape[0]
    t_total = n_tiles * tm
    slot = i % 2

    def issue_tile(tile, slot_):
        def issue(r, carry):
            for k in range(TOP_K):
                d = dest_ref[k * t_total + tile * tm + r]
                pltpu.make_async_copy(yp_hbm.at[pl.ds(d, 1), :], gbuf.at[slot_, k, pl.ds(r, 1), :],
                                      sem.at[slot_]).start(priority=k % 2)
            return carry

        lax.fori_loop(0, tm, issue, 0)

    @pl.when(i == 0)
    def _():
        issue_tile(0, 0)

    for s in range(2):
        @pl.when((i + 1 < n_tiles) & (slot == 1 - s))
        def _():
            issue_tile(i + 1, s)

    lo, hi = _unpack_bf16_pairs(hp_ref[...])
    h = jnp.concatenate([lo, hi], axis=1).astype(BF16)
    hid = (_silu(_dot(h, sg_ref[...])) * _dot(h, su_ref[...])).astype(BF16)
    out = _dot(hid, sd_ref[...])
    half = out.shape[1] // 2
    out_lo, out_hi = out[:, :half], out[:, half:]
    for k in range(TOP_K):
        pltpu.make_async_copy(yp_hbm.at[pl.ds(0, tm), :], gbuf.at[slot, k], sem.at[slot]).wait()
    gw = gw_ref[...]
    for k in range(TOP_K):
        lo, hi = _unpack_bf16_pairs(gbuf[slot, k])
        wk = gw[:, k:k + 1]
        out_lo = out_lo + wk * lo
        out_hi = out_hi + wk * hi
    out = jnp.concatenate([out_lo, out_hi], axis=1)
    r = DEEPNORM_ALPHA * x_ref[...] + mod_ref[5:6, :] * out
    o_ref[...] = _layer_norm(r, lng_ref[...], lnb_ref[...])


def _moe_combine(yp, dest, gw, hp, x2d, mod, layer, s_gate, s_up, s_down, ln_g, ln_b, rows_per_mod):
    t, d = x2d.shape
    tm = COMB_TM
    sf = s_gate.shape[2]
    tiles_per_mod = rows_per_mod // tm
    grid_spec = pltpu.PrefetchScalarGridSpec(
        num_scalar_prefetch=1,
        grid=(t // tm,),
        in_specs=[
            pl.BlockSpec(memory_space=pl.ANY),
            pl.BlockSpec((tm, TOP_K), lambda i, p: (i, 0)),
            pl.BlockSpec((tm, d // 2), lambda i, p: (i, 0)),
            pl.BlockSpec((tm, d), lambda i, p: (i, 0)),
            pl.BlockSpec((None, SUBLANES, d), lambda i, p: (i // tiles_per_mod, 0, 0)),
            pl.BlockSpec((None, d, sf), lambda i, p: (layer, 0, 0)),
            pl.BlockSpec((None, d, sf), lambda i, p: (layer, 0, 0)),
            pl.BlockSpec((None, sf, d), lambda i, p: (layer, 0, 0)),
            pl.BlockSpec((1, d), lambda i, p: (0, 0)),
            pl.BlockSpec((1, d), lambda i, p: (0, 0)),
        ],
        out_specs=pl.BlockSpec((tm, d), lambda i, p: (i, 0)),
        scratch_shapes=[pltpu.VMEM((2, TOP_K, tm, d // 2), jnp.uint32), pltpu.SemaphoreType.DMA((2,))],
    )
    return pl.pallas_call(
        _moe_combine_kernel,
        out_shape=jax.ShapeDtypeStruct((t, d), F32),
        grid_spec=grid_spec,
        compiler_params=_cparams("arbitrary"),
        name="moe_combine",
    )(dest, yp, gw, hp, x2d, mod, s_gate, s_up, s_down, ln_g.reshape(1, d), ln_b.reshape(1, d))


def _moe_layer(x2d, hp, logits_t, mod, rows_per_mod, layer, router_b, w_gate, w_up, w_down,
               s_gate, s_up, s_down, ln_g, ln_b):
    top_e, gw, rank, counts = _route(logits_t, router_b)
    dest, blk_e, n_used, pad_start, pad_len = _routing_plan(top_e, rank, counts)
    xs = _moe_dispatch(hp, dest, pad_start, pad_len, n_used, blk_e.shape[0] * MOE_TM)
    yp = _moe_ffn(xs, blk_e, n_used, layer, w_gate, w_up, w_down)
    return _moe_combine(yp, dest, gw.T, hp, x2d, mod, layer, s_gate, s_up, s_down, ln_g, ln_b, rows_per_mod)


def _hy_inproj_kernel(x_ref, mod_ref, w_ref, o_ref):
    h = (x_ref[...] * (1.0 + mod_ref[1:2, :]) + mod_ref[0:1, :]).astype(BF16)
    o_ref[...] = _dot(h, w_ref[...])


def _hy_inproj(x2d, mod, w3_bf16, rows_per_mod):
    rows, d = x2d.shape
    n_out = w3_bf16.shape[0]
    tm = HY_TM
    tiles_per_mod = rows_per_mod // tm
    return pl.pallas_call(
        _hy_inproj_kernel,
        out_shape=jax.ShapeDtypeStruct((n_out, rows, d), F32),
        grid=(n_out, rows // tm),
        in_specs=[
            pl.BlockSpec((tm, d), lambda j, i: (i, 0)),
            pl.BlockSpec((None, SUBLANES, d), lambda j, i: (i // tiles_per_mod, 0, 0)),
            pl.BlockSpec((None, d, d), lambda j, i: (j, 0, 0)),
        ],
        out_specs=pl.BlockSpec((None, tm, d), lambda j, i: (j, i, 0)),
        compiler_params=_cparams("arbitrary", "arbitrary"),
        name="hyena_inproj",
    )(x2d, mod, w3_bf16)


def _short_conv_kernel(u_ref, w_ref, b_ref, o_ref):
    u = u_ref[...]
    seq = u.shape[0]
    t = lax.broadcasted_iota(jnp.int32, (seq, 1), 0)
    prev = jnp.where(t == 0, 0.0, pltpu.roll(u, 1, 0))
    nxt = jnp.where(t == seq - 1, 0.0, pltpu.roll(u, seq - 1, 0))
    o_ref[...] = prev * w_ref[0:1, :] + u * w_ref[1:2, :] + nxt * w_ref[2:3, :] + b_ref[...]


def _short_conv(u4, conv_w, conv_b):
    n_out, b, seq, d = u4.shape
    tc = CONV_TC
    return pl.pallas_call(
        _short_conv_kernel,
        out_shape=jax.ShapeDtypeStruct(u4.shape, F32),
        grid=(n_out, b, d // tc),
        in_specs=[
            pl.BlockSpec((None, None, seq, tc), lambda j, i, c: (j, i, 0, c)),
            pl.BlockSpec((None, SUBLANES, tc), lambda j, i, c: (j, 0, c)),
            pl.BlockSpec((None, 1, tc), lambda j, i, c: (j, 0, c)),
        ],
        out_specs=pl.BlockSpec((None, None, seq, tc), lambda j, i, c: (j, i, 0, c)),
        compiler_params=_cparams("arbitrary", "arbitrary", "arbitrary"),
        name="hyena_short_conv",
    )(u4, conv_w, conv_b)


def _filter_kernel(z_ref, t_ref, keep_ref, w1_ref, b1_ref, w2_ref, b2_ref, w3_ref, b3_ref, fq_ref,
                   w4_ref, dl_ref, o_ref):
    fq = fq_ref[...]
    a = jnp.sin(fq * (_dot3(z_ref[...], w1_ref[...]) + b1_ref[...]))
    a = jnp.sin(fq * (_dot3(a, w2_ref[...]) + b2_ref[...]))
    a = jnp.sin(fq * (_dot3(a, w3_ref[...]) + b3_ref[...]))
    hf = _dot3(a, w4_ref[...])
    o_ref[...] = hf * jnp.exp(-t_ref[...] * jnp.abs(dl_ref[...])) * keep_ref[...]


def _hyena_filter(seq, d, f_w1, f_b1, f_w2, f_b2, f_w3, f_b3, f_w4, f_freq):
    n = 2 * seq
    t = jnp.linspace(0.0, 1.0, seq, dtype=F32)[:, None]
    bands = (HY_EMB - 1) // 2
    w = (2.0 * math.pi / seq) * jnp.arange(seq, dtype=F32)
    fb = jnp.linspace(1e-4, bands - 1, bands, dtype=F32)
    fw = w[:, None] * fb[None, :]
    z = jnp.concatenate([t, jnp.cos(fw), -jnp.sin(fw)], axis=-1)
    m = jnp.arange(n)
    src = jnp.where(m < seq, m, jnp.clip(n - m, 0, seq - 1))
    wpad = HY_FILTER_WIDTH - HY_EMB
    z_full = jnp.pad(z[src], ((0, 0), (0, wpad)))
    t_full = t[src]
    keep = (m != seq).astype(F32)[:, None]
    w1p = jnp.pad(f_w1.astype(F32), ((0, wpad), (0, 0)))
    deltas = jnp.linspace(math.log(HY_DECAY_TARGET) / HY_DECAY_FAST,
                          math.log(HY_DECAY_TARGET) / HY_DECAY_SLOW, d, dtype=F32)[None, :]
    fw_ = HY_FILTER_WIDTH
    tiles_per_dir = seq // FILT_TM
    row = lambda v: v.astype(F32).reshape(1, -1)
    small = lambda shape: pl.BlockSpec(shape, lambda i, o: (0, 0))
    return pl.pallas_call(
        _filter_kernel,
        out_shape=jax.ShapeDtypeStruct((HY_ORDER, n, d), F32),
        grid=(n // FILT_TM, HY_ORDER),
        in_specs=[
            pl.BlockSpec((FILT_TM, fw_), lambda i, o: (i, 0)),
            pl.BlockSpec((FILT_TM, 1), lambda i, o: (i, 0)),
            pl.BlockSpec((FILT_TM, 1), lambda i, o: (i, 0)),
            small((fw_, fw_)), small((1, fw_)), small((fw_, fw_)), small((1, fw_)),
            small((fw_, fw_)), small((1, fw_)), small((1, fw_)),
            pl.BlockSpec((fw_, d), lambda i, o: (0, (i // tiles_per_dir) * HY_ORDER + o)),
            small((1, d)),
        ],
        out_specs=pl.BlockSpec((None, FILT_TM, d), lambda i, o: (o, i, 0)),
        compiler_params=_cparams("arbitrary", "arbitrary"),
        name="hyena_filter",
    )(z_full, t_full, keep, w1p, row(f_b1), f_w2.astype(F32), row(f_b2), f_w3.astype(F32), row(f_b3),
      row(f_freq), f_w4.astype(F32), deltas)


def _dft_tables():
    n = FFT_N1 * FFT_N2
    k1 = np.arange(FFT_K1)
    n1 = np.arange(FFT_N1)
    ang1 = 2.0 * np.pi * ((k1[:, None] * n1[None, :]) % FFT_N1) / FFT_N1
    f1 = np.zeros((2 * FFT_K1P, FFT_N1), np.float32)
    f1[:FFT_K1] = np.cos(ang1)
    f1[FFT_K1P:FFT_K1P + FFT_K1] = -np.sin(ang1)
    coef = np.where((k1 == 0) | (k1 == FFT_N1 // 2), 1.0, 2.0) / n
    g = np.zeros((FFT_N1 // 2, 2 * FFT_K1P), np.float32)
    ang1h = ang1[:, :FFT_N1 // 2].T
    g[:, :FFT_K1] = np.cos(ang1h) * coef
    g[:, FFT_K1P:FFT_K1P + FFT_K1] = -np.sin(ang1h) * coef
    n2 = np.arange(FFT_N2)
    k2 = np.arange(FFT_N2)
    mf = np.zeros((FFT_K1P, 2 * FFT_N2, 2 * FFT_N2), np.float32)
    mi = np.zeros((FFT_K1P, 2 * FFT_N2, 2 * FFT_N2), np.float32)
    for a in range(FFT_K1):
        k = a + FFT_N1 * k2
        ang = 2.0 * np.pi * ((k[:, None] * n2[None, :]) % n) / n
        er, ei = np.cos(ang), -np.sin(ang)
        mf[a] = np.block([[er, -ei], [ei, er]])
        mi[a] = np.block([[er.T, ei.T], [-ei.T, er.T]])
    return f1, g, mf, mi


def _dft_stage1_kernel(x_ref, f_ref, re_ref, im_ref):
    n1 = x_ref.shape[0] // FFT_N2
    f = f_ref[...]

    def body(g, carry):
        base = g * FFT_NB
        rhs = jnp.concatenate([x_ref[pl.ds(base + j, n1, stride=FFT_N2), :] for j in range(FFT_NB)], axis=1)
        y = _dot(f, rhs.astype(BF16))
        for j in range(FFT_NB):
            yj = y[:, j * LANES:(j + 1) * LANES]
            re_ref[pl.ds(base + j, FFT_K1P, stride=FFT_N2), :] = yj[:FFT_K1P]
            im_ref[pl.ds(base + j, FFT_K1P, stride=FFT_N2), :] = yj[FFT_K1P:]
        return carry

    lax.fori_loop(0, FFT_N2 // FFT_NB, body, 0, unroll=2)


def _dft_stage1(x4, lead, f1):
    _, b, rows, d = x4.shape
    spec_o = pl.BlockSpec((None, FFT_K1P * FFT_N2, LANES), lambda i, c: (i, 0, c))
    return pl.pallas_call(
        _dft_stage1_kernel,
        out_shape=(jax.ShapeDtypeStruct((b, FFT_K1P * FFT_N2, d), F32),) * 2,
        grid=(b, d // LANES),
        in_specs=[pl.BlockSpec((None, None, rows, LANES), lambda i, c: (lead, i, 0, c)),
                  pl.BlockSpec(f1.shape, lambda i, c: (0, 0))],
        out_specs=(spec_o, spec_o),
        compiler_params=_cparams("arbitrary", "arbitrary"),
        name="dft_stage1",
    )(x4, f1)


def _dft_stage2_kernel(ar_ref, ai_ref, mf_ref, xr_ref, xi_ref):
    a = jnp.concatenate([ar_ref[...], ai_ref[...]], axis=0).astype(BF16)
    x = _dot(mf_ref[...], a)
    xr_ref[...] = x[:FFT_N2]
    xi_ref[...] = x[FFT_N2:]


def _dft_stage2(ar, ai, mf):
    b, _, n2, d = ar.shape
    spec = pl.BlockSpec((None, None, n2, d), lambda k, i: (i, k, 0, 0))
    return pl.pallas_call(
        _dft_stage2_kernel,
        out_shape=(jax.ShapeDtypeStruct(ar.shape, F32),) * 2,
        grid=(FFT_K1P, b),
        in_specs=[spec, spec, pl.BlockSpec((None, 2 * n2, 2 * n2), lambda k, i: (k, 0, 0))],
        out_specs=(spec, spec),
        compiler_params=_cparams("arbitrary", "arbitrary"),
        name="dft_stage2",
    )(ar, ai, mf)


def _spectral_kernel(ar_ref, ai_ref, hr_ref, hi_ref, mf_ref, mi_ref, br_ref, bi_ref):
    a = jnp.concatenate([ar_ref[...], ai_ref[...]], axis=0).astype(BF16)
    x = _dot(mf_ref[...], a)
    xr, xi = x[:FFT_N2], x[FFT_N2:]
    hr, hi = hr_ref[...], hi_ref[...]
    y = jnp.concatenate([xr * hr - xi * hi, xr * hi + xi * hr], axis=0).astype(BF16)
    bv = _dot(mi_ref[...], y)
    br_ref[...] = bv[:FFT_N2]
    bi_ref[...] = bv[FFT_N2:]


def _spectral_multiply(ar, ai, hr, hi, order, mf, mi):
    b, _, n2, d = ar.shape
    spec = pl.BlockSpec((None, None, n2, d), lambda k, i: (i, k, 0, 0))
    spec_h = pl.BlockSpec((None, None, n2, d), lambda k, i: (order, k, 0, 0))
    spec_m = pl.BlockSpec((None, 2 * n2, 2 * n2), lambda k, i: (k, 0, 0))
    return pl.pallas_call(
        _spectral_kernel,
        out_shape=(jax.ShapeDtypeStruct(ar.shape, F32),) * 2,
        grid=(FFT_K1P, b),
        in_specs=[spec, spec, spec_h, spec_h, spec_m, spec_m],
        out_specs=(spec, spec),
        compiler_params=_cparams("arbitrary", "arbitrary"),
        name="dft_spectral",
    )(ar, ai, hr, hi, mf, mi)


def _idft_gate_kernel(br_ref, bi_ref, gr_ref, gi_ref, z_ref, gate_ref, bias_ref, o_ref):
    n1h = z_ref.shape[0] // FFT_N2
    gr, gi = gr_ref[...], gi_ref[...]
    bias = bias_ref[...]

    def body(g, carry):
        base = g * FFT_NB
        rows = lambda ref, j, n: ref[pl.ds(base + j, n, stride=FFT_N2), :]
        br = jnp.concatenate([rows(br_ref, j, FFT_K1P) for j in range(FFT_NB)], axis=1).astype(BF16)
        bi = jnp.concatenate([rows(bi_ref, j, FFT_K1P) for j in range(FFT_NB)], axis=1).astype(BF16)
        y = _dot(gr, br) + _dot(gi, bi)
        for j in range(FFT_NB):
            yj = y[:, j * LANES:(j + 1) * LANES]
            o_ref[pl.ds(base + j, n1h, stride=FFT_N2), :] = (
                rows(gate_ref, j, n1h) * (yj + rows(z_ref, j, n1h) * bias))
        return carry

    lax.fori_loop(0, FFT_N2 // FFT_NB, body, 0, unroll=2)


def _idft_gate(br, bi, gr, gi, z4, z_lead, gate4, gate_lead, bias):
    _, b, seq, d = z4.shape
    spec_b = pl.BlockSpec((None, FFT_K1P * FFT_N2, LANES), lambda i, c: (i, 0, c))
    spec_t = pl.BlockSpec(gr.shape, lambda i, c: (0, 0))
    lead_spec = lambda lead: pl.BlockSpec((None, None, seq, LANES), lambda i, c: (lead, i, 0, c))
    return pl.pallas_call(
        _idft_gate_kernel,
        out_shape=jax.ShapeDtypeStruct((1, b, seq, d), F32),
        grid=(b, d // LANES),
        in_specs=[spec_b, spec_b, spec_t, spec_t, lead_spec(z_lead), lead_spec(gate_lead),
                  pl.BlockSpec((1, LANES), lambda i, c: (0, c))],
        out_specs=lead_spec(0),
        compiler_params=_cparams("arbitrary", "arbitrary"),
        name="idft_gate",
    )(br, bi, gr, gi, z4, gate4, bias)


def _hyena_mixer(x2d, mod, batch, seq, w_in, conv_w, conv_b, f_w1, f_b1, f_w2, f_b2, f_w3, f_b3, f_w4,
                 f_freq, bias_d):
    d = x2d.shape[1]
    assert 2 * seq == FFT_N1 * FFT_N2
    n1h = FFT_N1 // 2
    w3 = w_in.reshape(d, 3, d).transpose(1, 0, 2).astype(BF16)
    u = _hy_inproj(x2d, mod, w3, seq)
    cw = jnp.pad(conv_w.reshape(3, 3, d).transpose(1, 0, 2), ((0, 0), (0, SUBLANES - 3), (0, 0)))
    cb = conv_b.reshape(3, 1, d)
    u4 = _short_conv(u.reshape(3, batch, seq, d), cw, cb)
    f1, g, mf, mi = _dft_tables()
    f1 = jnp.asarray(f1).astype(BF16)
    g = jnp.asarray(g).astype(BF16)
    gr, gi = g[:, :FFT_K1P], g[:, FFT_K1P:]
    mf = jnp.asarray(mf).astype(BF16)
    mi = jnp.asarray(mi).astype(BF16)
    rows4 = lambda a: a.reshape(a.shape[0], FFT_K1P, FFT_N2, d)
    rows3 = lambda a: a.reshape(a.shape[0], FFT_K1P * FFT_N2, d)
    kern = _hyena_filter(seq, d, f_w1, f_b1, f_w2, f_b2, f_w3, f_b3, f_w4, f_freq)
    kr, ki = _dft_stage1(kern[None], 0, f1)
    hr, hi = _dft_stage2(rows4(kr), rows4(ki), mf)
    z4, z_lead = u4, 0
    f1h = f1[:, :n1h]
    for o in range(HY_ORDER):
        ar, ai = _dft_stage1(z4, z_lead, f1h)
        br, bi = _spectral_multiply(rows4(ar), rows4(ai), hr, hi, o, mf, mi)
        z4 = _idft_gate(rows3(br), rows3(bi), gr, gi, z4, z_lead, u4, 1 + o,
                        bias_d[o].astype(F32).reshape(1, d))
        z_lead = 0
    return z4.reshape(batch * seq, d)


def _mod_table(m_layer, d):
    m6 = m_layer.reshape(m_layer.shape[0], 6, d)
    return jnp.pad(m6, ((0, 0), (0, SUBLANES - 6), (0, 0)))


def kernel(x, c, ctx, c_ctx, ada_w, ada_b, ln_g, ln_b, attn_w_in, attn_w_out, attn_sink, attn_q_gain, attn_k_gain, hy_w_in, hy_conv_w, hy_conv_b, hy_f_w1, hy_f_b1, hy_f_w2, hy_f_b2, hy_f_w3, hy_f_b3, hy_f_w4, hy_f_freq, hy_bias_d, hy_w_out, moe_router_w, moe_router_b, moe_w_gate, moe_w_up, moe_w_down, moe_s_gate, moe_s_up, moe_s_down):
    batch, seq, d = x.shape
    n_ctx = ctx.shape[1]
    depth = ada_w.shape[0]
    assert depth == DEPTH and HEAD_DIM == LANES and batch + 1 <= SUBLANES
    cc = jnp.concatenate([c, c_ctx[None, :], jnp.zeros((SUBLANES - batch - 1, d), c.dtype)], axis=0)
    m_all = _ada_modulation(cc.astype(F32), ada_w, ada_b)
    x2d = x.reshape(batch * seq, d)
    cos_t, sin_t = _rope_tables(seq)
    s_gate, s_up, s_down = moe_s_gate.astype(BF16), moe_s_up.astype(BF16), moe_s_down.astype(BF16)
    for l in range(depth):
        mod = _mod_table(m_all[l], d)
        i = l // 2
        router_wt = moe_router_w[l].astype(F32).T
        if l % 2 == 0:
            w_in = attn_w_in[i].astype(BF16)
            qkv = _attn_inproj(x2d, mod, w_in, cos_t, sin_t, attn_q_gain[i], attn_k_gain[i],
                               seq, 0, INPROJ_TM)
            ones = jnp.ones((n_ctx, HEAD_DIM), F32)
            qkv_ctx = _attn_inproj(ctx.reshape(batch * n_ctx, d), mod, w_in, ones, jnp.zeros_like(ones),
                                   attn_q_gain[i], attn_k_gain[i], batch * n_ctx, batch, n_ctx)
            qkv = qkv.reshape(batch, seq, -1)
            qkv_ctx = qkv_ctx.reshape(batch, n_ctx, -1)
            o_a = _win_attention(qkv, qkv_ctx, attn_sink[i].astype(F32))
            o_b = _dense_attention(qkv, qkv_ctx)
            w_out = attn_w_out[i].astype(BF16)
            na = A_Q_HEADS * HEAD_DIM
            acts = [o_a.reshape(batch * seq, na), o_b.reshape(batch * seq, -1)]
            weights = [w_out[:na], w_out[na:]]
        else:
            z = _hyena_mixer(x2d, mod, batch, seq, hy_w_in[i], hy_conv_w[i], hy_conv_b[i], hy_f_w1[i],
                             hy_f_b1[i], hy_f_w2[i], hy_f_b2[i], hy_f_w3[i], hy_f_b3[i], hy_f_w4[i],
                             hy_f_freq[i], hy_bias_d[i])
            acts = [z]
            weights = [hy_w_out[i].astype(BF16)]
        x2d, hp, logits_t = _outproj_ln(acts, weights, x2d, mod, ln_g[l, 0], ln_b[l, 0], router_wt, seq)
        x2d = _moe_layer(x2d, hp, logits_t, mod, seq, l, moe_router_b[l], moe_w_gate, moe_w_up, moe_w_down,
                         s_gate, s_up, s_down, ln_g[l, 1], ln_b[l, 1])
    return x2d.reshape(batch, seq, d)
```
